```python
import math
import jax
import jax.numpy as jnp
from jax import lax
import numpy as np

D_MODEL = 4096
BATCH = 4
SEQ = 2048
DEPTH = 4
DEC_BATCH = 128
DEC_SEQ = 8
PAST_LEN = 8192
PAGE_SIZE = 128

N_MIXERS = 3
N_A = (DEPTH + 2) // N_MIXERS
N_B = (DEPTH + 1) // N_MIXERS
N_C = DEPTH // N_MIXERS
D_FF = 2 * D_MODEL
N_ADA = 9
EPS = 1e-6
H_A = D_MODEL // 128
DK_A = 128
DV_A = 128
D_A = H_A * DK_A
CONV_A = 4
DELTA_CHUNK = 64
N_IN_A = 3 * D_A + H_A * DV_A + 2 * H_A
D_CONV = D_MODEL
CONV_B = 31
H_C = D_MODEL // 128
Q_LORA = D_MODEL // 4
KV_LORA = D_MODEL // 8
NOPE = 128
ROPE = 64
V_DIM = 128
QK_DIM = NOPE + ROPE
ATTN_SCALE = QK_DIM ** -0.5
ROPE_THETA = 10000.0
Q_BLOCK = 128

kernel_name = 'hybrid_deltanet_conformer_mla_step'


def rms_norm(x, g):
    xf = x.astype(jnp.float32)
    y = xf * lax.rsqrt(jnp.mean(xf * xf, axis=-1, keepdims=True) + EPS)
    return (y * g.astype(jnp.float32)).astype(x.dtype)


def layer_norm(x, g, b):
    xf = x.astype(jnp.float32)
    xc = xf - jnp.mean(xf, axis=-1, keepdims=True)
    y = xc * lax.rsqrt(jnp.mean(xc * xc, axis=-1, keepdims=True) + EPS)
    return (y * g.astype(jnp.float32) + b.astype(jnp.float32)).astype(x.dtype)


def l2_norm(x):
    return x * lax.rsqrt(jnp.sum(x * x, axis=-1, keepdims=True) + EPS)


def ada_params(c, w, b):
    m = jax.nn.silu(c) @ w + b
    return jnp.split(m[:, None, :], N_ADA, axis=-1)


def ada_norm(x, g, shift, scale):
    return rms_norm(x, g) * (1 + scale) + shift


def swiglu(h, wg, wu, wd):
    return (jax.nn.silu(h @ wg) * (h @ wu)) @ wd


def causal_dwconv(x_ext, w):
    return lax.conv_general_dilated(x_ext, w[:, None, :].astype(x_ext.dtype), (1,), 'VALID',
                                    dimension_numbers=('NWC', 'WIO', 'NWC'),
                                    feature_group_count=x_ext.shape[-1])


def rope(x, pos):
    inv = ROPE_THETA ** (-jnp.arange(0, ROPE, 2, dtype=jnp.float32) / ROPE)
    ang = pos.astype(jnp.float32)[:, None] * inv[None, :]
    cos = jnp.cos(ang)[None, :, None, :]
    sin = jnp.sin(ang)[None, :, None, :]
    xf = x.astype(jnp.float32)
    x1, x2 = xf[..., :ROPE // 2], xf[..., ROPE // 2:]
    return jnp.concatenate([x1 * cos - x2 * sin, x2 * cos + x1 * sin], axis=-1).astype(x.dtype)


def tied_gain(g):
    return jnp.concatenate([g[:NOPE], g[NOPE:], g[NOPE:]])


def gated_delta_chunked(q, k, v, g, beta, s0):
    B, T, H, _ = q.shape
    C = min(DELTA_CHUNK, T)
    pad = (-T) % C
    if pad:
        def padt(t):
            return jnp.pad(t, [(0, 0), (0, pad)] + [(0, 0)] * (t.ndim - 2))
        q, k, v, g, beta = padt(q), padt(k), padt(v), padt(g), padt(beta)
    n = (T + pad) // C

    def chunks(t):
        t = t.reshape((B, n, C) + t.shape[2:])
        return jnp.moveaxis(t, (1, 3), (0, 2))

    q = chunks(q) * (DK_A ** -0.5)
    k, v, g, beta = chunks(k), chunks(v), chunks(g), chunks(beta)
    gc = jnp.cumsum(g, axis=-1)
    idx = jnp.arange(C)
    incl = idx[:, None] >= idx[None, :]
    strict = idx[:, None] > idx[None, :]
    decay = jnp.exp(jnp.where(incl, gc[..., :, None] - gc[..., None, :], -jnp.inf))
    kb = k * beta[..., None]
    m = jnp.where(strict, jnp.einsum('nbhid,nbhjd->nbhij', kb, k) * decay, 0.0)
    lhs = jnp.eye(C, dtype=jnp.float32) + m
    rhs = jnp.concatenate([v * beta[..., None], kb * jnp.exp(gc)[..., None]], axis=-1)
    sol = lax.linalg.triangular_solve(jnp.broadcast_to(lhs, m.shape), rhs, left_side=True,
                                      lower=True, unit_diagonal=True)
    u, w = sol[..., :DV_A], sol[..., DV_A:]
    attn = jnp.einsum('nbhid,nbhjd->nbhij', q, k) * decay

    def step(s, xs):
        qi, ki, ui, wi, ai, gi = xs
        dlt = ui - jnp.einsum('bhck,bhkv->bhcv', wi, s)
        o = (jnp.einsum('bhck,bhkv->bhcv', qi * jnp.exp(gi)[..., None], s)
             + jnp.einsum('bhij,bhjv->bhiv', ai, dlt))
        gl = gi[..., -1:]
        s = s * jnp.exp(gl)[..., None] + jnp.einsum('bhck,bhcv->bhkv', ki * jnp.exp(gl - gi)[..., None], dlt)
        return s, o

    s, o = lax.scan(step, s0, (q, k, u, w, attn, gc))
    o = jnp.moveaxis(o, (0, 2), (1, 3)).reshape(B, n * C, H, DV_A)[:, :T]
    return o, s


def deltanet_mixer(h, s0, conv0, w_in, conv_w, a_log, dt_bias, norm_g, w_out):
    B, T, _ = h.shape
    f32 = jnp.float32
    proj = h @ w_in
    o_z = 3 * D_A
    o_b = o_z + H_A * DV_A
    qkv = proj[..., :o_z]
    z = proj[..., o_z:o_b].reshape(B, T, H_A, DV_A)
    b = proj[..., o_b:o_b + H_A]
    a = proj[..., o_b + H_A:]
    qkv_ext = jnp.concatenate([conv0.astype(qkv.dtype), qkv], axis=1)
    new_conv = qkv_ext[:, -(CONV_A - 1):]
    qkv = jax.nn.silu(causal_dwconv(qkv_ext, conv_w)).astype(f32)
    q = l2_norm(qkv[..., :D_A].reshape(B, T, H_A, DK_A))
    k = l2_norm(qkv[..., D_A:2 * D_A].reshape(B, T, H_A, DK_A))
    v = qkv[..., 2 * D_A:].reshape(B, T, H_A, DV_A)
    beta = jax.nn.sigmoid(b.astype(f32))
    g = -jnp.exp(a_log.astype(f32)) * jax.nn.softplus(a.astype(f32) + dt_bias.astype(f32))
    o, s = gated_delta_chunked(q, k, v, g, beta, s0.astype(f32))
    o = rms_norm(o, norm_g) * jax.nn.silu(z.astype(f32))
    out = o.reshape(B, T, H_A * DV_A).astype(h.dtype) @ w_out
    return out, s.astype(h.dtype), new_conv


def conformer_conv_mixer(h, conv0, w_pw1, b_pw1, w_dw, b_dw, ln_g, ln_b, w_pw2, b_pw2):
    a = h @ w_pw1 + b_pw1
    u = a[..., :D_CONV] * jax.nn.sigmoid(a[..., D_CONV:])
    u_ext = jnp.concatenate([conv0.astype(u.dtype), u], axis=1)
    y = causal_dwconv(u_ext, w_dw) + b_dw
    y = jax.nn.silu(layer_norm(y, ln_g, ln_b))
    return y @ w_pw2 + b_pw2, u_ext[:, -(CONV_B - 1):]


def mla_qkv(h, pos, w_dq, qa_g, w_uq, w_dkv, kva_g, q_norm_g):
    B, T, _ = h.shape
    cq = rms_norm(h @ w_dq, qa_g)
    q = (cq @ w_uq).reshape(B, T, H_C, QK_DIM)
    q = jnp.concatenate([q[..., :NOPE], rope(q[..., NOPE:], pos)], axis=-1)
    q = rms_norm(q, tied_gain(q_norm_g))
    kv = h @ w_dkv
    ckv = rms_norm(kv[..., :KV_LORA], kva_g)
    krope = rope(kv[..., None, KV_LORA:], pos)[:, :, 0]
    return q, ckv, krope


def mla_keys(ckv, krope, w_uk, k_norm_g):
    k_nope = jnp.einsum('...lc,chd->...lhd', ckv, w_uk)
    k_r = jnp.broadcast_to(krope[..., None, :], k_nope.shape[:-1] + (ROPE,))
    return rms_norm(jnp.concatenate([k_nope, k_r], axis=-1), tied_gain(k_norm_g))


def mla_attend_prompt(q, k, v):
    B, S, H, _ = q.shape
    nq = S // Q_BLOCK
    qb = q.reshape(B, nq, Q_BLOCK, H, QK_DIM).swapaxes(0, 1)
    kpos = jnp.arange(S)

    def block(args):
        qi, i = args
        s = jnp.einsum('bqhd,bkhd->bhqk', qi, k).astype(jnp.float32) * ATTN_SCALE
        qpos = i * Q_BLOCK + jnp.arange(Q_BLOCK)
        s = jnp.where(kpos[None, :] <= qpos[:, None], s, -jnp.inf)
        p = jax.nn.softmax(s, axis=-1).astype(v.dtype)
        return jnp.einsum('bhqk,bkhd->bqhd', p, v)

    o = lax.map(block, (qb, jnp.arange(nq)))
    return o.swapaxes(0, 1).reshape(B, S, H, V_DIM)


def mla_attend_sample(q, ckv_new, kr_new, pool_ckv, pool_kr, layer, page_table, w_uk, w_uv, k_norm_g):
    T = q.shape[1]
    past = page_table.shape[1] * PAGE_SIZE
    mask = jnp.arange(past + T)[None, :] <= (past + jnp.arange(T))[:, None]

    def one(args):
        pt, qs, cn, kn = args
        c_all = jnp.concatenate([pool_ckv[layer, pt].reshape(past, KV_LORA).astype(cn.dtype), cn], axis=0)
        r_all = jnp.concatenate([pool_kr[layer, pt].reshape(past, ROPE).astype(kn.dtype), kn], axis=0)
        k = mla_keys(c_all, r_all, w_uk, k_norm_g)
        s = jnp.einsum('qhd,khd->hqk', qs, k).astype(jnp.float32) * ATTN_SCALE
        s = jnp.where(mask, s, -jnp.inf)
        p = jax.nn.softmax(s, axis=-1).astype(c_all.dtype)
        o_lat = jnp.einsum('hqk,kc->qhc', p, c_all)
        return jnp.einsum('qhc,chd->qhd', o_lat, w_uv)

    return lax.map(one, (page_table, q, ckv_new, kr_new))


def split_ukv(w_ukv):
    w = w_ukv.reshape(KV_LORA, H_C, NOPE + V_DIM)
    return w[..., :NOPE], w[..., NOPE:]


def setup_inputs(seed: int = 0) -> dict:
    key = jax.random.key(seed)
    keys = iter(jax.random.split(key, 64))
    f32 = jnp.float32

    def nrm(shape, scale=1.0):
        return jax.random.normal(next(keys), shape, f32) * scale

    def gain(shape):
        return 1.0 + nrm(shape, 0.02)

    n_pages = PAST_LEN // PAGE_SIZE
    n_used = DEC_BATCH * n_pages
    n_pool = n_used + max(1, n_used // 4)
    page_table = jax.random.permutation(next(keys), n_pool)[:n_used].reshape(DEC_BATCH, n_pages).astype(jnp.int32)
    a_log = jnp.log(jax.random.uniform(next(keys), (N_A, H_A), f32, 1.0, 16.0))
    dt = jnp.exp(jax.random.uniform(next(keys), (N_A, H_A), f32, math.log(1e-3), math.log(1e-1)))
    dt_bias = dt + jnp.log(-jnp.expm1(-dt))
    D = D_MODEL
    return {
        'x_prompt': nrm((BATCH, SEQ, D)),
        'x_sample': nrm((DEC_BATCH, DEC_SEQ, D)),
        'cache_ckv': nrm((N_C, n_pool, PAGE_SIZE, KV_LORA)),
        'cache_krope': nrm((N_C, n_pool, PAGE_SIZE, ROPE)),
        'state_delta': nrm((N_A, DEC_BATCH, H_A, DK_A, DV_A), 0.1),
        'state_delta_conv': nrm((N_A, DEC_BATCH, CONV_A - 1, 3 * D_A)),
        'state_conv': nrm((N_B, DEC_BATCH, CONV_B - 1, D_CONV), 0.5),
        'page_table': page_table,
        'c_prompt': nrm((BATCH, D)),
        'c_sample': nrm((DEC_BATCH, D)),
        'ada_w': nrm((DEPTH, D, N_ADA * D), 0.5 * D ** -0.5),
        'ada_b': nrm((DEPTH, N_ADA * D), 0.02),
        'norm_g': gain((DEPTH, 3, D)),
        'ffn_wg': nrm((DEPTH, 2, D, D_FF), D ** -0.5),
        'ffn_wu': nrm((DEPTH, 2, D, D_FF), D ** -0.5),
        'ffn_wd': nrm((DEPTH, 2, D_FF, D), D_FF ** -0.5),
        'dn_w_in': nrm((N_A, D, N_IN_A), D ** -0.5),
        'dn_conv_w': nrm((N_A, CONV_A, 3 * D_A), CONV_A ** -0.5),
        'dn_a_log': a_log,
        'dn_dt_bias': dt_bias,
        'dn_norm_g': gain((N_A, DV_A)),
        'dn_w_out': nrm((N_A, H_A * DV_A, D), (H_A * DV_A) ** -0.5),
        'cv_w_pw1': nrm((N_B, D, 2 * D_CONV), D ** -0.5),
        'cv_b_pw1': nrm((N_B, 2 * D_CONV), 0.02),
        'cv_w_dw': nrm((N_B, CONV_B, D_CONV), CONV_B ** -0.5),
        'cv_b_dw': nrm((N_B, D_CONV), 0.02),
        'cv_ln_g': gain((N_B, D_CONV)),
        'cv_ln_b': nrm((N_B, D_CONV), 0.02),
        'cv_w_pw2': nrm((N_B, D_CONV, D), D_CONV ** -0.5),
        'cv_b_pw2': nrm((N_B, D), 0.02),
        'm_w_dq': nrm((N_C, D, Q_LORA), D ** -0.5),
        'm_qa_g': gain((N_C, Q_LORA)),
        'm_w_uq': nrm((N_C, Q_LORA, H_C * QK_DIM), Q_LORA ** -0.5),
        'm_w_dkv': nrm((N_C, D, KV_LORA + ROPE), D ** -0.5),
        'm_kva_g': gain((N_C, KV_LORA)),
        'm_w_ukv': nrm((N_C, KV_LORA, H_C * (NOPE + V_DIM)), KV_LORA ** -0.5),
        'm_q_norm_g': gain((N_C, NOPE + ROPE // 2)),
        'm_k_norm_g': gain((N_C, NOPE + ROPE // 2)),
        'm_w_o': nrm((N_C, H_C * V_DIM, D), (H_C * V_DIM) ** -0.5),
    }


def reference(x_prompt, x_sample, cache_ckv, cache_krope, state_delta, state_delta_conv, state_conv,
              page_table, c_prompt, c_sample, ada_w, ada_b, norm_g, ffn_wg, ffn_wu, ffn_wd,
              dn_w_in, dn_conv_w, dn_a_log, dn_dt_bias, dn_norm_g, dn_w_out,
              cv_w_pw1, cv_b_pw1, cv_w_dw, cv_b_dw, cv_ln_g, cv_ln_b, cv_w_pw2, cv_b_pw2,
              m_w_dq, m_qa_g, m_w_uq, m_w_dkv, m_kva_g, m_w_ukv, m_q_norm_g, m_k_norm_g, m_w_o):
    xp, xs = x_prompt, x_sample
    bp, sp = xp.shape[0], xp.shape[1]
    pos_p = jnp.arange(sp)
    pos_s = page_table.shape[1] * PAGE_SIZE + jnp.arange(xs.shape[1])
    ckv_p, kr_p, ckv_s, kr_s = [], [], [], []
    dn_p, dn_s, dnc_p, dnc_s = [], [], [], []
    cv_p, cv_s = [], []
    for l in range(DEPTH):
        mp = ada_params(c_prompt, ada_w[l], ada_b[l])
        ms = ada_params(c_sample, ada_w[l], ada_b[l])
        xp = xp + 0.5 * mp[2] * swiglu(ada_norm(xp, norm_g[l, 0], mp[0], mp[1]), ffn_wg[l, 0], ffn_wu[l, 0], ffn_wd[l, 0])
        xs = xs + 0.5 * ms[2] * swiglu(ada_norm(xs, norm_g[l, 0], ms[0], ms[1]), ffn_wg[l, 0], ffn_wu[l, 0], ffn_wd[l, 0])
        hp = ada_norm(xp, norm_g[l, 1], mp[3], mp[4])
        hs = ada_norm(xs, norm_g[l, 1], ms[3], ms[4])
        kind = l % N_MIXERS
        i = l // N_MIXERS
        if kind == 0:
            s0 = jnp.zeros((bp, H_A, DK_A, DV_A), jnp.float32)
            c0 = jnp.zeros((bp, CONV_A - 1, 3 * D_A), hp.dtype)
            op, s_new_p, c_new_p = deltanet_mixer(hp, s0, c0, dn_w_in[i], dn_conv_w[i], dn_a_log[i],
                                                  dn_dt_bias[i], dn_norm_g[i], dn_w_out[i])
            os_, s_new_s, c_new_s = deltanet_mixer(hs, state_delta[i], state_delta_conv[i], dn_w_in[i], dn_conv_w[i],
                                                   dn_a_log[i], dn_dt_bias[i], dn_norm_g[i], dn_w_out[i])
            dn_p.append(s_new_p)
            dn_s.append(s_new_s)
            dnc_p.append(c_new_p)
            dnc_s.append(c_new_s)
        elif kind == 1:
            c0 = jnp.zeros((bp, CONV_B - 1, D_CONV), hp.dtype)
            op, b_new_p = conformer_conv_mixer(hp, c0, cv_w_pw1[i], cv_b_pw1[i], cv_w_dw[i], cv_b_dw[i],
                                               cv_ln_g[i], cv_ln_b[i], cv_w_pw2[i], cv_b_pw2[i])
            os_, b_new_s = conformer_conv_mixer(hs, state_conv[i], cv_w_pw1[i], cv_b_pw1[i], cv_w_dw[i], cv_b_dw[i],
                                                cv_ln_g[i], cv_ln_b[i], cv_w_pw2[i], cv_b_pw2[i])
            cv_p.append(b_new_p)
            cv_s.append(b_new_s)
        else:
            w_uk, w_uv = split_ukv(m_w_ukv[i])
            qp, ckvp, krp = mla_qkv(hp, pos_p, m_w_dq[i], m_qa_g[i], m_w_uq[i], m_w_dkv[i], m_kva_g[i], m_q_norm_g[i])
            kp = mla_keys(ckvp, krp, w_uk, m_k_norm_g[i])
            vp = jnp.einsum('blc,chd->blhd', ckvp, w_uv)
            attp = mla_attend_prompt(qp, kp, vp)
            op = attp.reshape(bp, sp, H_C * V_DIM) @ m_w_o[i]
            qs, ckvs, krs = mla_qkv(hs, pos_s, m_w_dq[i], m_qa_g[i], m_w_uq[i], m_w_dkv[i], m_kva_g[i], m_q_norm_g[i])
            atts = mla_attend_sample(qs, ckvs, krs, cache_ckv, cache_krope, i, page_table, w_uk, w_uv, m_k_norm_g[i])
            os_ = atts.reshape(xs.shape[0], xs.shape[1], H_C * V_DIM) @ m_w_o[i]
            ckv_p.append(ckvp)
            kr_p.append(krp)
            ckv_s.append(ckvs)
            kr_s.append(krs)
        xp = xp + mp[5] * op
        xs = xs + ms[5] * os_
        xp = xp + 0.5 * mp[8] * swiglu(ada_norm(xp, norm_g[l, 2], mp[6], mp[7]), ffn_wg[l, 1], ffn_wu[l, 1], ffn_wd[l, 1])
        xs = xs + 0.5 * ms[8] * swiglu(ada_norm(xs, norm_g[l, 2], ms[6], ms[7]), ffn_wg[l, 1], ffn_wu[l, 1], ffn_wd[l, 1])
    new_ckv_prompt = jnp.stack(ckv_p)
    new_krope_prompt = jnp.stack(kr_p)
    new_ckv_sample = jnp.stack(ckv_s)
    new_krope_sample = jnp.stack(kr_s)
    new_delta_prompt = jnp.stack(dn_p)
    new_delta_sample = jnp.stack(dn_s)
    new_delta_conv_prompt = jnp.stack(dnc_p)
    new_delta_conv_sample = jnp.stack(dnc_s)
    new_conv_prompt = jnp.stack(cv_p)
    new_conv_sample = jnp.stack(cv_s)
    return (xp, xs, new_ckv_prompt, new_krope_prompt, new_ckv_sample, new_krope_sample,
            new_delta_prompt, new_delta_sample, new_delta_conv_prompt, new_delta_conv_sample,
            new_conv_prompt, new_conv_sample)
```

```python
import collections
import functools

import jax
import jax.numpy as jnp
from jax import lax
from jax.experimental import pallas as pl
from jax.experimental.pallas import tpu as pltpu

F32 = jnp.float32
BF16 = jnp.bfloat16
EPS = 1e-6
N_ADA = 9
ROPE_THETA = 10000.0
LANES = 128
VMEM_LIMIT = 52 * 1024 * 1024
NEG = -1e30

Cfg = collections.namedtuple(
    "Cfg", "batch seq dbatch dseq d m_prompt m n_pages page heads dk nope rope kv_lora q_lora")


def _cp(*sem):
    return pltpu.CompilerParams(dimension_semantics=sem, vmem_limit_bytes=VMEM_LIMIT)


def _pick(dim, pref, mult=LANES):
    if dim <= pref:
        return dim
    t = (pref // mult) * mult
    while t >= mult:
        if dim % t == 0:
            return t
        t -= mult
    return dim


def _pick_tm(cfg, pref):
    g = cfg.dbatch
    t = (pref // g) * g
    while t > g:
        if cfg.seq % t == 0 and (cfg.dseq * g) % t == 0:
            return t
        t -= g
    return g


def _rowblk(cfg, tm):
    return lambda i: jnp.minimum((i * tm) // cfg.seq, cfg.batch)


def _bdot(a, b):
    return jnp.dot(a.astype(BF16), b.astype(BF16), preferred_element_type=F32)


def _bdot_nt(a, b):
    return lax.dot_general(a.astype(BF16), b.astype(BF16), (((1,), (1,)), ((), ())),
                           preferred_element_type=F32)


def _silu(x):
    return x * jax.nn.sigmoid(x)


def _mm(a, ws, *, grid_m, n_blocks, tm, tn, tk, outs, epilogue, extras=(), a_silu=False, row_off=0):
    nk = a.shape[1] // tk
    nw, ne, no = len(ws), len(extras), len(outs)
    in_specs = [pl.BlockSpec((tm, tk), lambda i, j, k: (i + row_off, k))]
    args = [a]
    for w, lead, off in ws:
        in_specs.append(pl.BlockSpec((None,) * len(lead) + (tk, tn),
                                     lambda i, j, k, lead=lead, off=off: lead + (k, j + off)))
        args.append(w)
    for arr, blk, imap in extras:
        in_specs.append(pl.BlockSpec(blk, lambda i, j, k, imap=imap: imap(i, j)))
        args.append(arr)
    out_specs = [pl.BlockSpec(blk, lambda i, j, k, imap=imap: imap(i, j)) for _, _, blk, imap in outs]
    out_shape = [jax.ShapeDtypeStruct(s, d) for s, d, _, _ in outs]

    def body(*refs):
        a_ref = refs[0]
        w_refs = refs[1:1 + nw]
        e_refs = refs[1 + nw:1 + nw + ne]
        o_refs = refs[1 + nw + ne:1 + nw + ne + no]
        acc_refs = refs[1 + nw + ne + no:]
        av = a_ref[...]
        if a_silu:
            av = _silu(av.astype(F32))
        av = av.astype(BF16)
        parts = [jnp.dot(av, w[...].astype(BF16), preferred_element_type=F32) for w in w_refs]
        if nk == 1:
            epilogue(parts, e_refs, o_refs)
            return
        k = pl.program_id(2)

        @pl.when(k == 0)
        def _():
            for acc, p in zip(acc_refs, parts):
                acc[...] = p

        @pl.when(k > 0)
        def _():
            for acc, p in zip(acc_refs, parts):
                acc[...] += p

        @pl.when(k == nk - 1)
        def _():
            epilogue([acc[...] for acc in acc_refs], e_refs, o_refs)

    scratch = [pltpu.VMEM((tm, tn), F32) for _ in range(nw)] if nk > 1 else []
    res = pl.pallas_call(
        body, grid=(grid_m, n_blocks, nk), in_specs=in_specs, out_specs=out_specs,
        out_shape=out_shape, scratch_shapes=scratch,
        compiler_params=_cp("parallel", "parallel", "arbitrary"))(*args)
    return res


def _ep_store(accs, e_refs, o_refs):
    o_refs[0][...] = accs[0].astype(o_refs[0].dtype)


def _ep_swiglu(accs, e_refs, o_refs):
    o_refs[0][...] = (_silu(accs[0]) * accs[1]).astype(o_refs[0].dtype)


def _ep_glu(accs, e_refs, o_refs):
    a = accs[0] + e_refs[0][...]
    b = accs[1] + e_refs[1][...]
    o_refs[0][...] = (a * jax.nn.sigmoid(b)).astype(o_refs[0].dtype)


def _ep_resid(coef, has_bias, accs, e_refs, o_refs):
    y = accs[0]
    if has_bias:
        y = y + e_refs[2][...]
    g = e_refs[1][...]
    tm, tn = y.shape
    gy = (y.reshape(tm // g.shape[0], g.shape[0], tn) * g[None]).reshape(tm, tn)
    o_refs[0][...] = e_refs[0][...] + coef * gy


def _plain_mm(cfg, a, w, lead, *, n, col_off=0, out_dtype=F32, tm=1024, tn=512, tk=1024):
    m = a.shape[0]
    tm = _pick(m, tm, 8)
    tn = _pick(n, tn)
    tk = _pick(a.shape[1], tk)
    return _mm(a, [(w, lead, col_off // tn)], grid_m=m // tm, n_blocks=n // tn, tm=tm, tn=tn, tk=tk,
               outs=[((m, n), out_dtype, (tm, tn), lambda i, j: (i, j))], epilogue=_ep_store)[0]


def _resid_mm(cfg, a, w, lead, x, mod, gate_idx, coef, bias=None, bias_lead=(), tm=1024, tn=1024, tk=1024):
    m, d = x.shape
    tm = _pick_tm(cfg, tm)
    tn = _pick(d, tn)
    tk = _pick(a.shape[1], tk)
    rb = _rowblk(cfg, tm)
    goff = gate_idx * (d // tn)
    extras = [(x, (tm, tn), lambda i, j: (i, j)),
              (mod, (cfg.dbatch, tn), lambda i, j: (rb(i), goff + j))]
    if bias is not None:
        extras.append((bias, (None,) * len(bias_lead) + (1, tn), lambda i, j: bias_lead + (0, j)))
    return _mm(a, [(w, lead, 0)], grid_m=m // tm, n_blocks=d // tn, tm=tm, tn=tn, tk=tk,
               outs=[((m, d), F32, (tm, tn), lambda i, j: (i, j))], extras=extras,
               epilogue=functools.partial(_ep_resid, coef, bias is not None))[0]


def _adanorm(cfg, x, norm_g3, gidx, mod, sub):
    m, d = x.shape
    tm = _pick_tm(cfg, 256)
    g = cfg.dbatch
    rb = _rowblk(cfg, tm)

    def body(x_ref, g_ref, sh_ref, sc_ref, o_ref):
        xv = x_ref[...]
        y = xv * lax.rsqrt(jnp.mean(xv * xv, axis=-1, keepdims=True) + EPS) * g_ref[...]
        y3 = y.reshape(tm // g, g, d) * (1.0 + sc_ref[...])[None] + sh_ref[...][None]
        o_ref[...] = y3.reshape(tm, d).astype(BF16)

    return pl.pallas_call(
        body, grid=(m // tm,),
        in_specs=[pl.BlockSpec((tm, d), lambda i: (i, 0)),
                  pl.BlockSpec((None, 1, d), lambda i: (gidx, 0, 0)),
                  pl.BlockSpec((g, d), lambda i: (rb(i), 3 * sub)),
                  pl.BlockSpec((g, d), lambda i: (rb(i), 3 * sub + 1))],
        out_specs=pl.BlockSpec((tm, d), lambda i: (i, 0)),
        out_shape=jax.ShapeDtypeStruct((m, d), BF16),
        compiler_params=_cp("parallel"))(x, norm_g3, mod, mod)


def _layernorm_silu(y, ln_g, ln_b, lead):
    m, c = y.shape
    tm = _pick(m, 256, 8)

    def body(y_ref, g_ref, b_ref, o_ref):
        v = y_ref[...]
        vc = v - jnp.mean(v, axis=-1, keepdims=True)
        n = vc * lax.rsqrt(jnp.mean(vc * vc, axis=-1, keepdims=True) + EPS) * g_ref[...] + b_ref[...]
        o_ref[...] = _silu(n).astype(BF16)

    return pl.pallas_call(
        body, grid=(m // tm,),
        in_specs=[pl.BlockSpec((tm, c), lambda i: (i, 0)),
                  pl.BlockSpec((None, 1, c), lambda i: (lead, 0, 0)),
                  pl.BlockSpec((None, 1, c), lambda i: (lead, 0, 0))],
        out_specs=pl.BlockSpec((tm, c), lambda i: (i, 0)),
        out_shape=jax.ShapeDtypeStruct((m, c), BF16),
        compiler_params=_cp("parallel"))(y, ln_g, ln_b)


def _dwconv(x, init, w, w_lead, *, n_ch, row_off_blocks, batch, rows, tstride, tt, tc, post, out_dtype):
    taps = w.shape[-2]
    halo = (taps - 1) * tstride
    hp = init.shape[1]
    nt = rows // tt
    ncb = n_ch // tc

    def body(x_ref, init_ref, w_ref, o_ref, ext_ref):
        t = pl.program_id(2)

        @pl.when(t == 0)
        def _():
            ext_ref[0:hp, :] = init_ref[...]

        @pl.when(t > 0)
        def _():
            ext_ref[0:hp, :] = ext_ref[tt:tt + hp, :]

        ext_ref[hp:hp + tt, :] = x_ref[...]
        wv = w_ref[...]
        acc = None
        for j in range(taps):
            o = hp - halo + j * tstride
            term = ext_ref[o:o + tt, :] * wv[j:j + 1, :]
            acc = term if acc is None else acc + term
        o_ref[...] = post(acc, pl.program_id(1)).astype(out_dtype)

    return pl.pallas_call(
        body, grid=(batch, ncb, nt),
        in_specs=[pl.BlockSpec((tt, tc), lambda b, c, t: (row_off_blocks + b * nt + t, c)),
                  pl.BlockSpec((None, hp, tc), lambda b, c, t: (b, 0, c)),
                  pl.BlockSpec((None,) * len(w_lead) + (taps, tc), lambda b, c, t: w_lead + (0, c))],
        out_specs=pl.BlockSpec((tt, tc), lambda b, c, t: (b * nt + t, c)),
        out_shape=jax.ShapeDtypeStruct((batch * rows, n_ch), out_dtype),
        scratch_shapes=[pltpu.VMEM((hp + tt, tc), F32)],
        compiler_params=_cp("parallel", "parallel", "arbitrary"))(x, init, w)


def _dwconv_both(cfg, x, w, w_lead, n_ch, init_prompt, init_sample, post, out_dtype, tt_p, tc):
    tt_p = _pick(cfg.seq, tt_p, 8)
    tc = _pick(n_ch, tc)
    ms = cfg.dseq * cfg.dbatch
    yp = _dwconv(x, init_prompt, w, w_lead, n_ch=n_ch, row_off_blocks=0, batch=cfg.batch, rows=cfg.seq,
                 tstride=1, tt=tt_p, tc=tc, post=post, out_dtype=out_dtype)
    assert cfg.m_prompt % ms == 0
    ys = _dwconv(x, init_sample, w, w_lead, n_ch=n_ch, row_off_blocks=cfg.m_prompt // ms, batch=1, rows=ms,
                 tstride=cfg.dbatch, tt=ms, tc=tc, post=post, out_dtype=out_dtype)
    return yp, ys


def _halo_inits(cfg, state, taps, n_ch):
    h = taps - 1
    hp_p = -(-h // 8) * 8
    init_p = jnp.zeros((cfg.batch, hp_p, n_ch), F32)
    init_s = jnp.swapaxes(state.astype(F32), 0, 1).reshape(1, h * cfg.dbatch, n_ch)
    return init_p, init_s


def _delta_rule(qkv, z, ba, s0, s0_lead, gparams, norm_g, *, heads, seg, n_states, steps):
    rows = qkv.shape[0]
    r = LANES
    nseg = r // seg
    assert n_states in (1, nseg) and (n_states == 1 or steps == 1) and (n_states == nseg or nseg == 1)
    groups = rows // (r * steps)
    z_arr, z_off = z

    def body(q_ref, k_ref, v_ref, z_ref, ba_ref, s0_ref, gp_ref, ng_ref, o_ref, sf_ref, s_ref):
        h = pl.program_id(1)
        ci = pl.program_id(2)

        @pl.when(ci == 0)
        def _():
            s_ref[...] = s0_ref[...].astype(F32)

        q = q_ref[...]
        k = k_ref[...]
        v = v_ref[...]
        bav = ba_ref[...]
        gp = gp_ref[...]
        lane = lax.broadcasted_iota(jnp.int32, (r, LANES), 1)
        beta_all = jax.nn.sigmoid(bav)
        xs = bav + gp[1:2, :]
        softplus = jnp.maximum(xs, 0.0) + jnp.log1p(jnp.exp(-jnp.abs(xs)))
        g_all = -jnp.exp(gp[0:1, :]) * softplus
        beta = jnp.sum(jnp.where(lane == h, beta_all, 0.0), axis=-1, keepdims=True)
        g = jnp.sum(jnp.where(lane == h + heads, g_all, 0.0), axis=-1, keepdims=True)

        row = lax.broadcasted_iota(jnp.int32, (r, r), 0)
        col = lax.broadcasted_iota(jnp.int32, (r, r), 1)
        same = (row // seg) == (col // seg)
        incl = jnp.logical_and(same, row >= col)
        strict = jnp.logical_and(same, row > col)
        gc = jnp.sum(jnp.where(incl, jnp.broadcast_to(g, (r, r)).T, 0.0), axis=-1, keepdims=True)
        gi = jnp.broadcast_to(gc, (r, r))
        gj = gi.T
        decay = jnp.where(incl, jnp.exp(jnp.where(incl, gi - gj, 0.0)), 0.0)
        kb = k * beta
        m_mat = jnp.where(strict, _bdot_nt(kb, k) * decay, 0.0)
        t_inv = jnp.where(row == col, 1.0, 0.0)
        s = 1
        while s < seg:
            rb = row // s
            off = jnp.logical_and(rb % 2 == 1, (col // s) == rb - 1)
            m_off = jnp.where(off, m_mat, 0.0)
            t_inv = t_inv - _bdot(_bdot(t_inv, m_off), t_inv)
            s *= 2
        u = _bdot(t_inv, v * beta)
        w = _bdot(t_inv, kb * jnp.exp(gi))
        attn = _bdot_nt(q, k) * decay
        qg = q * jnp.exp(gi)
        last = jnp.logical_and(same, col % seg == seg - 1)
        gl = jnp.sum(jnp.where(last, gj, 0.0), axis=-1, keepdims=True)
        kd_t = (k * jnp.exp(gl - gc)).T
        egl = jnp.exp(gl)
        if n_states == 1:
            st = s_ref[0]
            dlt = u - _bdot(w, st)
            o = _bdot(qg, st) + _bdot(attn, dlt)
            s_ref[0] = st * egl[r - 1:r, :] + _bdot(kd_t, dlt)
        else:
            wq = []
            for si in range(nseg):
                lo = si * seg
                both = jnp.concatenate([w[lo:lo + seg], qg[lo:lo + seg]], axis=0)
                wq.append(_bdot(both, s_ref[si]))
            dlt = u - jnp.concatenate([x[:seg] for x in wq], axis=0)
            o = jnp.concatenate([x[seg:] for x in wq], axis=0) + _bdot(attn, dlt)
            rseg = lax.broadcasted_iota(jnp.int32, (r, LANES), 0) // seg
            for si in range(nseg):
                lo = si * seg
                upd = _bdot(kd_t, jnp.where(rseg == si, dlt, 0.0))
                s_ref[si] = s_ref[si] * egl[lo:lo + 1, :] + upd
        on = o * lax.rsqrt(jnp.mean(o * o, axis=-1, keepdims=True) + EPS) * ng_ref[...]
        o_ref[...] = (on * _silu(z_ref[...])).astype(BF16)

        @pl.when(ci == steps - 1)
        def _():
            sf_ref[...] = s_ref[...]

    hh = heads
    ns = n_states
    return pl.pallas_call(
        body, grid=(groups, heads, steps),
        in_specs=[pl.BlockSpec((r, LANES), lambda b, h, ci: (b * steps + ci, h)),
                  pl.BlockSpec((r, LANES), lambda b, h, ci: (b * steps + ci, hh + h)),
                  pl.BlockSpec((r, LANES), lambda b, h, ci: (b * steps + ci, 2 * hh + h)),
                  pl.BlockSpec((r, LANES), lambda b, h, ci: (b * steps + ci, z_off + h)),
                  pl.BlockSpec((r, LANES), lambda b, h, ci: (b * steps + ci, 0)),
                  pl.BlockSpec((None,) * len(s0_lead) + (ns, None, LANES, LANES),
                               lambda b, h, ci: s0_lead + (b, h, 0, 0)),
                  pl.BlockSpec((8, LANES), lambda b, h, ci: (0, 0)),
                  pl.BlockSpec((1, LANES), lambda b, h, ci: (0, 0))],
        out_specs=[pl.BlockSpec((r, LANES), lambda b, h, ci: (b * steps + ci, h)),
                   pl.BlockSpec((ns, None, LANES, LANES), lambda b, h, ci: (b, h, 0, 0))],
        out_shape=[jax.ShapeDtypeStruct((rows, heads * LANES), BF16),
                   jax.ShapeDtypeStruct((groups * ns, heads, LANES, LANES), F32)],
        scratch_shapes=[pltpu.VMEM((ns, LANES, LANES), F32)],
        compiler_params=_cp("parallel", "parallel", "arbitrary"))(
            qkv, qkv, qkv, z_arr, ba, s0, gparams, norm_g)


def _rope_tables(cfg):
    half = cfg.rope // 2
    inv = ROPE_THETA ** (-jnp.arange(0, cfg.rope, 2, dtype=F32) / cfg.rope)
    pos = jnp.concatenate([jnp.tile(jnp.arange(cfg.seq), cfg.batch),
                           jnp.repeat(cfg.n_pages * cfg.page + jnp.arange(cfg.dseq), cfg.dbatch)])
    ang = pos.astype(F32)[:, None] * inv[None, :]
    cos, sin = jnp.cos(ang), jnp.sin(ang)
    zpad = jnp.zeros((cfg.m, LANES - cfg.rope), F32)
    zh = jnp.zeros((cfg.m, half), F32)
    cosf = jnp.concatenate([cos, cos, zpad], axis=1)
    sin_a = jnp.concatenate([-sin, zh, zpad], axis=1)
    sin_b = jnp.concatenate([zh, sin, zpad], axis=1)
    return cosf, sin_a, sin_b


def _rope128(x, cosf, sin_a, sin_b, half):
    return (x * cosf + pltpu.roll(x, LANES - half, 1) * sin_a + pltpu.roll(x, half, 1) * sin_b)


def _mla_cq(cfg, h, w_dq, qa_g3, i):
    m = h.shape[0]
    n = cfg.q_lora
    tm = _pick(m, 1024, 8)
    tk = _pick(cfg.d, 512)

    def ep(accs, e_refs, o_refs):
        y = accs[0]
        y = y * lax.rsqrt(jnp.mean(y * y, axis=-1, keepdims=True) + EPS) * e_refs[0][...]
        o_refs[0][...] = y.astype(BF16)

    return _mm(h, [(w_dq, (i,), 0)], grid_m=m // tm, n_blocks=1, tm=tm, tn=n, tk=tk,
               outs=[((m, n), BF16, (tm, n), lambda i_, j: (i_, 0))],
               extras=[(qa_g3, (None, 1, n), lambda i_, j: (i, 0, 0))], epilogue=ep)[0]


def _mla_q(cfg, cq, w_uq_p, gq, gk, tables):
    m = cq.shape[0]
    hw = 2 * LANES
    tm = _pick(m, 1024, 8)
    qk_dim = cfg.nope + cfg.rope
    half = cfg.rope // 2
    scale = qk_dim ** -0.5

    def ep(accs, e_refs, o_refs):
        cosf, sa, sb, gq_r, gk_r = [r[...] for r in e_refs]
        acc = accs[0]
        rot = _rope128(acc[:, LANES:], cosf, sa, sb, half)
        qf = jnp.concatenate([acc[:, :LANES], rot], axis=-1)
        ssq = jnp.sum(qf * qf, axis=-1, keepdims=True) * (1.0 / qk_dim)
        o_refs[0][...] = (qf * lax.rsqrt(ssq + EPS) * gq_r * (gk_r * scale)).astype(BF16)

    tab = [(t, (tm, LANES), lambda i, j: (i, 0)) for t in tables]
    gains = [(g, (1, hw), lambda i, j: (0, 0)) for g in (gq, gk)]
    return _mm(cq, [(w_uq_p, (), 0)], grid_m=m // tm, n_blocks=cfg.heads, tm=tm, tn=hw, tk=cfg.q_lora,
               outs=[((m, cfg.heads * hw), BF16, (tm, hw), lambda i, j: (i, j))],
               extras=tab + gains, epilogue=ep)[0]


def _mla_ckv(cfg, h, w_dkv_p, kva_g3, i, tables):
    m = h.shape[0]
    kv = cfg.kv_lora
    n = kv + LANES
    tm = _pick(m, 512, 8)
    tk = _pick(cfg.d, 512)
    half = cfg.rope // 2

    def ep(accs, e_refs, o_refs):
        cosf, sa, sb, g = [r[...] for r in e_refs]
        acc = accs[0]
        c = acc[:, :kv]
        o_refs[0][...] = c * lax.rsqrt(jnp.mean(c * c, axis=-1, keepdims=True) + EPS) * g
        o_refs[1][...] = _rope128(acc[:, kv:], cosf, sa, sb, half)

    tab = [(t, (tm, LANES), lambda i_, j: (i_, 0)) for t in tables]
    return _mm(h, [(w_dkv_p, (), 0)], grid_m=m // tm, n_blocks=1, tm=tm, tn=n, tk=tk,
               outs=[((m, kv), F32, (tm, kv), lambda i_, j: (i_, 0)),
                     ((m, LANES), F32, (tm, LANES), lambda i_, j: (i_, 0))],
               extras=tab + [(kva_g3, (None, 1, kv), lambda i_, j: (i, 0, 0))], epilogue=ep)


def _mla_prompt_kv(cfg, ckv, krope_p, w_ukv, i):
    hw = 2 * LANES
    mp = cfg.m_prompt
    tm = _pick(mp, 1024, 8)
    qk_dim = cfg.nope + cfg.rope

    def ep(accs, e_refs, o_refs):
        acc = accs[0]
        kr = e_refs[0][...]
        kn = acc[:, :LANES]
        ssq = (jnp.sum(kn * kn, axis=-1, keepdims=True) + jnp.sum(kr * kr, axis=-1, keepdims=True))
        s = lax.rsqrt(ssq * (1.0 / qk_dim) + EPS)
        o_refs[0][...] = jnp.concatenate([kn * s, kr * s], axis=-1).astype(BF16)
        o_refs[1][...] = acc[:, LANES:].astype(BF16)

    return _mm(ckv, [(w_ukv, (i,), 0)], grid_m=mp // tm, n_blocks=cfg.heads, tm=tm, tn=hw, tk=cfg.kv_lora,
               outs=[((mp, cfg.heads * hw), BF16, (tm, hw), lambda i_, j: (i_, j)),
                     ((mp, cfg.heads * LANES), BF16, (tm, LANES), lambda i_, j: (i_, j))],
               extras=[(krope_p, (tm, LANES), lambda i_, j: (i_, 0))], epilogue=ep)


def _flash_prompt(cfg, q, k, v):
    tq = _pick(cfg.seq, 512, 8)
    nq = cfg.seq // tq
    hw = 2 * LANES

    def body(q_ref, k_ref, v_ref, o_ref, m_ref, l_ref, acc_ref):
        qi = pl.program_id(2)
        kj = pl.program_id(3)

        @pl.when(kj == 0)
        def _():
            m_ref[...] = jnp.full(m_ref.shape, NEG, F32)
            l_ref[...] = jnp.zeros(l_ref.shape, F32)
            acc_ref[...] = jnp.zeros(acc_ref.shape, F32)

        @pl.when(kj <= qi)
        def _():
            s = lax.dot_general(q_ref[...], k_ref[...], (((1,), (1,)), ((), ())), preferred_element_type=F32)
            qpos = qi * tq + lax.broadcasted_iota(jnp.int32, (tq, tq), 0)
            kpos = kj * tq + lax.broadcasted_iota(jnp.int32, (tq, tq), 1)
            s = jnp.where(kpos <= qpos, s, NEG)
            m_prev = m_ref[...]
            m_new = jnp.maximum(m_prev, jnp.max(s, axis=-1, keepdims=True))
            alpha = jnp.exp(m_prev - m_new)
            p = jnp.exp(s - m_new)
            l_ref[...] = alpha * l_ref[...] + jnp.sum(p, axis=-1, keepdims=True)
            acc_ref[...] = alpha * acc_ref[...] + jnp.dot(p.astype(BF16), v_ref[...], preferred_element_type=F32)
            m_ref[...] = m_new

        @pl.when(kj == nq - 1)
        def _():
            o_ref[...] = (acc_ref[...] / l_ref[...]).astype(BF16)

    return pl.pallas_call(
        body, grid=(cfg.batch, cfg.heads, nq, nq),
        in_specs=[pl.BlockSpec((tq, hw), lambda b, h, qi, kj: (b * nq + qi, h)),
                  pl.BlockSpec((tq, hw), lambda b, h, qi, kj: (b * nq + jnp.minimum(kj, qi), h)),
                  pl.BlockSpec((tq, LANES), lambda b, h, qi, kj: (b * nq + jnp.minimum(kj, qi), h))],
        out_specs=pl.BlockSpec((tq, LANES), lambda b, h, qi, kj: (b * nq + qi, h)),
        out_shape=jax.ShapeDtypeStruct((cfg.m_prompt, cfg.heads * LANES), BF16),
        scratch_shapes=[pltpu.VMEM((tq, 1), F32), pltpu.VMEM((tq, 1), F32), pltpu.VMEM((tq, LANES), F32)],
        compiler_params=_cp("parallel", "parallel", "parallel", "arbitrary"))(q, k, v)


def _head_mm(a, w, *, a_block_of_head, kdim, n, out_dtype):
    rows = a.shape[0]
    heads = w.shape[0]

    def body(a_ref, w_ref, o_ref):
        o_ref[...] = jnp.dot(a_ref[...].astype(BF16), w_ref[...], preferred_element_type=F32).astype(out_dtype)

    return pl.pallas_call(
        body, grid=(heads,),
        in_specs=[pl.BlockSpec((rows, kdim), lambda h: (0, a_block_of_head(h))),
                  pl.BlockSpec((None, kdim, n), lambda h: (h, 0, 0))],
        out_specs=pl.BlockSpec((rows, n), lambda h: (0, h)),
        out_shape=jax.ShapeDtypeStruct((rows, heads * n), out_dtype),
        compiler_params=_cp("parallel"))(a, w)


def _decode_attention(cfg, page_table, cache_ckv, cache_krope, layer, qabs, qr, c_new, kr_new, w_uk_t):
    db, hq, kv = qabs.shape
    tq = cfg.dseq
    heads = cfg.heads
    npg = cfg.n_pages
    assert npg % 2 == 0 and hq == heads * tq and tq == 8
    npair = npg // 2
    page = cfg.page
    qk_dim = cfg.nope + cfg.rope
    hb_rows = min(heads, 8) * cfg.nope

    def body(pt_ref, c0_ref, c1_ref, r0_ref, r1_ref, qa_ref, qr_ref, cn_ref, rn_ref, wt_ref,
             o_ref, m_ref, l_ref, acc_ref, s_ref):
        p = pl.program_id(1)

        @pl.when(p == 0)
        def _():
            m_ref[...] = jnp.full(m_ref.shape, NEG, F32)
            l_ref[...] = jnp.zeros(l_ref.shape, F32)
            acc_ref[...] = jnp.zeros(acc_ref.shape, F32)

        def process(c, kr, new_rows):
            nkeys = c.shape[0]
            cb = c.astype(BF16)
            krb = kr.astype(BF16)
            ones = jnp.ones((8, cfg.rope), BF16)
            ssq_r = _bdot_nt(ones, kr * kr)[0:1, :]
            s_all = _bdot_nt(qa_ref[...], cb) + _bdot_nt(qr_ref[...], krb)
            for hb in range(heads * cfg.nope // hb_rows):
                kt = _bdot_nt(wt_ref[hb * hb_rows:(hb + 1) * hb_rows, :], cb)
                for hh in range(hb_rows // cfg.nope):
                    hd = hb * (hb_rows // cfg.nope) + hh
                    sl = kt[hh * cfg.nope:(hh + 1) * cfg.nope, :]
                    ssq = jnp.sum(sl * sl, axis=0, keepdims=True) + ssq_r
                    ksc = lax.rsqrt(ssq * (1.0 / qk_dim) + EPS)
                    s_ref[hd * tq:(hd + 1) * tq, 0:nkeys] = s_all[hd * tq:(hd + 1) * tq, :] * ksc
            s = s_ref[:, 0:nkeys]
            if new_rows:
                qrow = lax.broadcasted_iota(jnp.int32, (hq, nkeys), 0) % tq
                kcol = lax.broadcasted_iota(jnp.int32, (hq, nkeys), 1)
                s = jnp.where(kcol <= qrow, s, NEG)
            m_prev = m_ref[...]
            m_new = jnp.maximum(m_prev, jnp.max(s, axis=-1, keepdims=True))
            alpha = jnp.exp(m_prev - m_new)
            pr = jnp.exp(s - m_new)
            l_ref[...] = alpha * l_ref[...] + jnp.sum(pr, axis=-1, keepdims=True)
            acc_ref[...] = alpha * acc_ref[...] + jnp.dot(pr.astype(BF16), cb, preferred_element_type=F32)
            m_ref[...] = m_new

        @pl.when(p < npair)
        def _():
            process(jnp.concatenate([c0_ref[...], c1_ref[...]], axis=0),
                    jnp.concatenate([r0_ref[...], r1_ref[...]], axis=0), False)

        @pl.when(p == npair)
        def _():
            process(cn_ref[...], rn_ref[...], True)
            o_ref[...] = acc_ref[...] / l_ref[...]

    def pg(which):
        def imap(b, p, pt):
            pp = jnp.minimum(p, npair - 1)
            return (layer, pt[b * npg + 2 * pp + which], 0, 0)
        return imap

    grid_spec = pltpu.PrefetchScalarGridSpec(
        num_scalar_prefetch=1, grid=(db, npair + 1),
        in_specs=[pl.BlockSpec((None, None, page, kv), pg(0)),
                  pl.BlockSpec((None, None, page, kv), pg(1)),
                  pl.BlockSpec((None, None, page, cfg.rope), pg(0)),
                  pl.BlockSpec((None, None, page, cfg.rope), pg(1)),
                  pl.BlockSpec((None, hq, kv), lambda b, p, pt: (b, 0, 0)),
                  pl.BlockSpec((None, hq, cfg.rope), lambda b, p, pt: (b, 0, 0)),
                  pl.BlockSpec((None, LANES, kv), lambda b, p, pt: (b, 0, 0)),
                  pl.BlockSpec((None, LANES, cfg.rope), lambda b, p, pt: (b, 0, 0)),
                  pl.BlockSpec((heads * cfg.nope, kv), lambda b, p, pt: (0, 0))],
        out_specs=pl.BlockSpec((None, hq, kv), lambda b, p, pt: (b, 0, 0)),
        scratch_shapes=[pltpu.VMEM((hq, 1), F32), pltpu.VMEM((hq, 1), F32), pltpu.VMEM((hq, kv), F32),
                        pltpu.VMEM((hq, 2 * page), F32)])
    return pl.pallas_call(
        body, grid_spec=grid_spec, out_shape=jax.ShapeDtypeStruct((db, hq, kv), F32),
        compiler_params=_cp("parallel", "arbitrary"))(
            page_table.reshape(-1), cache_ckv, cache_ckv, cache_krope, cache_krope,
            qabs, qr, c_new, kr_new, w_uk_t)


def _to_batch_major(cfg, rows):
    return jnp.swapaxes(rows.reshape(cfg.dseq, cfg.dbatch, rows.shape[-1]), 0, 1)


def _to_time_major(cfg, arr):
    return jnp.swapaxes(arr, 0, 1).reshape(cfg.dseq * cfg.dbatch, arr.shape[-1])


def _ffn(cfg, x, mod, norm_g3, l, sub, which, wg, wu, wd):
    h = _adanorm(cfg, x, norm_g3, 3 * l + sub, mod, sub)
    m, d = x.shape
    dff = wg.shape[-1]
    tm = _pick(m, 1024, 8)
    tn = _pick(dff, 512)
    tk = _pick(d, 1024)
    act = _mm(h, [(wg, (l, which), 0), (wu, (l, which), 0)], grid_m=m // tm, n_blocks=dff // tn,
              tm=tm, tn=tn, tk=tk, outs=[((m, dff), BF16, (tm, tn), lambda i, j: (i, j))],
              epilogue=_ep_swiglu)[0]
    return _resid_mm(cfg, act, wd, (l, which), x, mod, 3 * sub + 2, 0.5)


def _deltanet_layer(cfg, h, i, state_delta, state_delta_conv, dn_w_in, dn_conv_w, dn_a_log, dn_dt_bias,
                    dn_norm_g, prompt_chunk):
    heads, dk = cfg.heads, cfg.dk
    d_a = heads * dk
    n_qkvz = 4 * d_a
    mp = cfg.m_prompt
    proj = _plain_mm(cfg, h, dn_w_in, (i,), n=n_qkvz)
    w_ba = jnp.pad(dn_w_in[i, :, n_qkvz:], ((0, 0), (0, LANES - 2 * heads)))
    ba = _plain_mm(cfg, h, w_ba, (), n=LANES, tn=LANES)

    nq = d_a
    tc = _pick(3 * d_a, 512)

    def post(y, cb):
        y = _silu(y)
        is_qk = cb < (2 * nq) // tc
        is_q = cb < nq // tc
        outs = []
        for s in range(tc // dk):
            ys = y[:, s * dk:(s + 1) * dk]
            ssq = jnp.sum(ys * ys, axis=-1, keepdims=True)
            sc = jnp.where(is_qk, lax.rsqrt(ssq + EPS), 1.0) * jnp.where(is_q, dk ** -0.5, 1.0)
            outs.append(ys * sc)
        return jnp.concatenate(outs, axis=-1)

    taps = dn_conv_w.shape[1]
    init_p, init_s = _halo_inits(cfg, state_delta_conv[i], taps, 3 * d_a)
    qkv_p, qkv_s = _dwconv_both(cfg, proj, dn_conv_w, (i,), 3 * d_a, init_p, init_s, post, F32, 256, tc)

    gparams = jnp.zeros((8, LANES), F32)
    gparams = gparams.at[0, heads:2 * heads].set(dn_a_log[i].astype(F32))
    gparams = gparams.at[1, heads:2 * heads].set(dn_dt_bias[i].astype(F32))
    ng = dn_norm_g[i].astype(F32).reshape(1, dk)

    s0_p = jnp.zeros((cfg.batch, heads, dk, dk), F32)
    o_p, s_p = _delta_rule(qkv_p, (proj, 3 * heads), ba, s0_p, (), gparams, ng,
                           heads=heads, seg=prompt_chunk, n_states=1, steps=cfg.seq // LANES)
    ms = cfg.dseq * cfg.dbatch
    o_s, s_s = _delta_rule(_to_batch_major(cfg, qkv_s).reshape(ms, 3 * d_a),
                           (_to_batch_major(cfg, proj[mp:, 3 * d_a:]).reshape(ms, d_a), 0),
                           _to_batch_major(cfg, ba[mp:]).reshape(ms, LANES), state_delta, (i,), gparams, ng,
                           heads=heads, seg=cfg.dseq, n_states=LANES // cfg.dseq, steps=1)
    o_all = jnp.concatenate([o_p, _to_time_major(cfg, o_s.reshape(cfg.dbatch, cfg.dseq, d_a))], axis=0)
    new_conv_p = proj[:mp].reshape(cfg.batch, cfg.seq, n_qkvz)[:, cfg.seq - (taps - 1):, :3 * d_a]
    ext_s = jnp.concatenate([state_delta_conv[i].astype(F32), _to_batch_major(cfg, proj[mp:, :3 * d_a])], axis=1)
    new_conv_s = ext_s[:, -(taps - 1):]
    return o_all, s_p, s_s, new_conv_p, new_conv_s


def _conformer_layer(cfg, h, i, state_conv, cv_w_pw1, cv_b_pw1, cv_w_dw, cv_b_dw, cv_ln_g, cv_ln_b):
    m, d = h.shape
    dc = cv_w_dw.shape[-1]
    tm = _pick(m, 1024, 8)
    tn = _pick(dc, 512)
    tk = _pick(d, 1024)
    b3 = cv_b_pw1.reshape(cv_b_pw1.shape[0], 1, 2 * dc)
    u = _mm(h, [(cv_w_pw1, (i,), 0), (cv_w_pw1, (i,), dc // tn)], grid_m=m // tm, n_blocks=dc // tn,
            tm=tm, tn=tn, tk=tk, outs=[((m, dc), F32, (tm, tn), lambda i_, j: (i_, j))],
            extras=[(b3, (None, 1, tn), lambda i_, j: (i, 0, j)),
                    (b3, (None, 1, tn), lambda i_, j: (i, 0, dc // tn + j))],
            epilogue=_ep_glu)[0]
    taps = cv_w_dw.shape[1]
    init_p, init_s = _halo_inits(cfg, state_conv[i], taps, dc)
    yp, ys = _dwconv_both(cfg, u, cv_w_dw, (i,), dc, init_p, init_s, lambda y, cb: y, F32, 128, 256)
    y = jnp.concatenate([yp, ys], axis=0) + cv_b_dw[i][None, :]
    hn = _layernorm_silu(y, cv_ln_g.reshape(-1, 1, dc), cv_ln_b.reshape(-1, 1, dc), i)
    mp = cfg.m_prompt
    new_p = u[:mp].reshape(cfg.batch, cfg.seq, dc)[:, cfg.seq - (taps - 1):]
    ext_s = jnp.concatenate([state_conv[i].astype(F32), _to_batch_major(cfg, u[mp:])], axis=1)
    return hn, new_p, ext_s[:, -(taps - 1):]


def _mla_layer(cfg, h, i, cache_ckv, cache_krope, page_table, tables, m_w_dq, m_qa_g, m_w_uq, m_w_dkv,
               m_kva_g, m_w_ukv, m_q_norm_g, m_k_norm_g):
    heads, nope, rope, kv = cfg.heads, cfg.nope, cfg.rope, cfg.kv_lora
    qk_dim = nope + rope
    hw = 2 * LANES
    mp = cfg.m_prompt
    half = rope // 2

    def tied(g):
        g = g.astype(F32)
        return jnp.concatenate([g[:nope], g[nope:], g[nope:], jnp.zeros((hw - qk_dim,), F32)]).reshape(1, hw)

    w_uq_p = jnp.pad(m_w_uq[i].reshape(cfg.q_lora, heads, qk_dim),
                     ((0, 0), (0, 0), (0, hw - qk_dim))).reshape(cfg.q_lora, heads * hw)
    w_dkv_p = jnp.pad(m_w_dkv[i], ((0, 0), (0, LANES - rope)))
    cq = _mla_cq(cfg, h, m_w_dq, m_qa_g.reshape(-1, 1, cfg.q_lora), i)
    q = _mla_q(cfg, cq, w_uq_p, tied(m_q_norm_g[i]), tied(m_k_norm_g[i]), tables)
    ckv, krope_p = _mla_ckv(cfg, h, w_dkv_p, m_kva_g.reshape(-1, 1, kv), i, tables)

    k_p, v_p = _mla_prompt_kv(cfg, ckv, krope_p, m_w_ukv, i)
    att_p = _flash_prompt(cfg, q, k_p, v_p)

    w3 = m_w_ukv[i].reshape(kv, heads, nope + LANES)
    w_uk_h = jnp.transpose(w3[:, :, :nope], (1, 2, 0)).astype(BF16)
    w_uv_h = jnp.transpose(w3[:, :, nope:], (1, 0, 2)).astype(BF16)
    q_s = q[mp:]
    ms = q_s.shape[0]
    qabs = _head_mm(q_s, w_uk_h, a_block_of_head=lambda hd: 2 * hd, kdim=nope, n=kv, out_dtype=BF16)
    qabs = jnp.transpose(qabs.reshape(cfg.dseq, cfg.dbatch, heads, kv), (1, 2, 0, 3)).reshape(
        cfg.dbatch, heads * cfg.dseq, kv)
    qr = q_s.reshape(cfg.dseq, cfg.dbatch, heads, hw)[..., nope:qk_dim]
    qr = jnp.transpose(qr, (1, 2, 0, 3)).reshape(cfg.dbatch, heads * cfg.dseq, rope)
    ckv_s = _to_batch_major(cfg, ckv[mp:])
    kr_s = _to_batch_major(cfg, krope_p[mp:, :rope])
    c_new = jnp.pad(ckv_s, ((0, 0), (0, LANES - cfg.dseq), (0, 0)))
    kr_new = jnp.pad(kr_s, ((0, 0), (0, LANES - cfg.dseq), (0, 0)))
    o_lat = _decode_attention(cfg, page_table, cache_ckv, cache_krope, i, qabs, qr, c_new, kr_new,
                              w_uk_h.reshape(heads * nope, kv))
    o_lat = jnp.transpose(o_lat.reshape(cfg.dbatch, heads, cfg.dseq, kv), (2, 0, 1, 3)).reshape(ms, heads * kv)
    att_s = _head_mm(o_lat, w_uv_h, a_block_of_head=lambda hd: hd, kdim=kv, n=LANES, out_dtype=BF16)
    att = jnp.concatenate([att_p, att_s], axis=0)
    ckv_p_out = ckv[:mp].reshape(cfg.batch, cfg.seq, kv)
    kr_p_out = krope_p[:mp, :rope].reshape(cfg.batch, cfg.seq, rope)
    return att, ckv_p_out, kr_p_out, ckv_s, kr_s


def kernel(x_prompt, x_sample, cache_ckv, cache_krope, state_delta, state_delta_conv, state_conv, page_table,
           c_prompt, c_sample, ada_w, ada_b, norm_g, ffn_wg, ffn_wu, ffn_wd, dn_w_in, dn_conv_w, dn_a_log,
           dn_dt_bias, dn_norm_g, dn_w_out, cv_w_pw1, cv_b_pw1, cv_w_dw, cv_b_dw, cv_ln_g, cv_ln_b, cv_w_pw2,
           cv_b_pw2, m_w_dq, m_qa_g, m_w_uq, m_w_dkv, m_kva_g, m_w_ukv, m_q_norm_g, m_k_norm_g, m_w_o):
    batch, seq, d = x_prompt.shape
    dbatch, dseq, _ = x_sample.shape
    depth = ada_w.shape[0]
    rope = cache_krope.shape[-1]
    kv_lora = cache_ckv.shape[-1]
    heads = dn_a_log.shape[-1]
    nope = m_q_norm_g.shape[-1] - rope // 2
    cfg = Cfg(batch=batch, seq=seq, dbatch=dbatch, dseq=dseq, d=d, m_prompt=batch * seq,
              m=batch * seq + dbatch * dseq, n_pages=page_table.shape[1], page=cache_ckv.shape[2],
              heads=heads, dk=state_delta.shape[-2], nope=nope, rope=rope, kv_lora=kv_lora,
              q_lora=m_w_dq.shape[-1])
    assert dbatch % 8 == 0 and seq % dbatch == 0 and cfg.dk == LANES and nope == LANES

    x = jnp.concatenate([x_prompt.reshape(batch * seq, d).astype(F32),
                         jnp.swapaxes(x_sample.astype(F32), 0, 1).reshape(dseq * dbatch, d)], axis=0)
    c_all = jnp.concatenate([jnp.repeat(c_prompt, dbatch, axis=0), c_sample], axis=0).astype(F32)
    norm_g3 = norm_g.reshape(depth * 3, 1, d).astype(F32)
    ada_b3 = ada_b.reshape(depth, 1, N_ADA * d)
    tables = _rope_tables(cfg)
    mrows = c_all.shape[0]
    tn_ada = _pick(N_ADA * d, 1024)
    tk_ada = _pick(d, 512)

    def ep_bias(accs, e_refs, o_refs):
        o_refs[0][...] = accs[0] + e_refs[0][...]

    outs = {k: [] for k in ("ckv_p", "kr_p", "ckv_s", "kr_s", "dn_p", "dn_s", "dnc_p", "dnc_s", "cv_p", "cv_s")}
    for l in range(depth):
        mod = _mm(c_all, [(ada_w, (l,), 0)], grid_m=1, n_blocks=N_ADA * d // tn_ada, tm=mrows, tn=tn_ada,
                  tk=tk_ada, outs=[((mrows, N_ADA * d), F32, (mrows, tn_ada), lambda i, j: (0, j))],
                  extras=[(ada_b3, (None, 1, tn_ada), lambda i, j, l=l: (l, 0, j))], epilogue=ep_bias,
                  a_silu=True)[0]
        x = _ffn(cfg, x, mod, norm_g3, l, 0, 0, ffn_wg, ffn_wu, ffn_wd)
        h = _adanorm(cfg, x, norm_g3, 3 * l + 1, mod, 1)
        kind, i = l % 3, l // 3
        if kind == 0:
            o, s_p, s_s, nc_p, nc_s = _deltanet_layer(cfg, h, i, state_delta, state_delta_conv, dn_w_in,
                                                      dn_conv_w, dn_a_log, dn_dt_bias, dn_norm_g,
                                                      min(LANES, seq))
            x = _resid_mm(cfg, o, dn_w_out, (i,), x, mod, 5, 1.0)
            outs["dn_p"].append(s_p)
            outs["dn_s"].append(s_s)
            outs["dnc_p"].append(nc_p)
            outs["dnc_s"].append(nc_s)
        elif kind == 1:
            hn, nb_p, nb_s = _conformer_layer(cfg, h, i, state_conv, cv_w_pw1, cv_b_pw1, cv_w_dw, cv_b_dw,
                                              cv_ln_g, cv_ln_b)
            x = _resid_mm(cfg, hn, cv_w_pw2, (i,), x, mod, 5, 1.0,
                          bias=cv_b_pw2.reshape(-1, 1, d), bias_lead=(i,))
            outs["cv_p"].append(nb_p)
            outs["cv_s"].append(nb_s)
        else:
            att, ckv_p, kr_p, ckv_s, kr_s = _mla_layer(cfg, h, i, cache_ckv, cache_krope, page_table, tables,
                                                       m_w_dq, m_qa_g, m_w_uq, m_w_dkv, m_kva_g, m_w_ukv,
                                                       m_q_norm_g, m_k_norm_g)
            x = _resid_mm(cfg, att, m_w_o, (i,), x, mod, 5, 1.0)
            outs["ckv_p"].append(ckv_p)
            outs["kr_p"].append(kr_p)
            outs["ckv_s"].append(ckv_s)
            outs["kr_s"].append(kr_s)
        x = _ffn(cfg, x, mod, norm_g3, l, 2, 1, ffn_wg, ffn_wu, ffn_wd)

    mp = cfg.m_prompt
    y_prompt = x[:mp].reshape(batch, seq, d)
    y_sample = jnp.swapaxes(x[mp:].reshape(dseq, dbatch, d), 0, 1)
    st = jnp.stack
    return (y_prompt, y_sample, st(outs["ckv_p"]), st(outs["kr_p"]), st(outs["ckv_s"]), st(outs["kr_s"]),
            st(outs["dn_p"]), st(outs["dn_s"]), st(outs["dnc_p"]), st(outs["dnc_s"]),
            st(outs["cv_p"]), st(outs["cv_s"]))
```

```python
import collections
import functools

import jax
import jax.numpy as jnp
from jax import lax
from jax.experimental import pallas as pl
from jax.experimental.pallas import tpu as pltpu

F32 = jnp.float32
BF16 = jnp.bfloat16
EPS = 1e-6
N_ADA = 9
ROPE_THETA = 10000.0
LANES = 128
VMEM_LIMIT = 52 * 1024 * 1024
NEG = -1e30

Cfg = collections.namedtuple(
    "Cfg", "batch seq dbatch dseq d m_prompt m n_pages page heads dk nope rope kv_lora q_lora")


def _cp(*sem):
    return pltpu.CompilerParams(dimension_semantics=sem, vmem_limit_bytes=VMEM_LIMIT)


def _pick(dim, pref, mult=LANES):
    if dim <= pref:
        return dim
    t = (pref // mult) * mult
    while t >= mult:
        if dim % t == 0:
            return t
        t -= mult
    return dim


def _pick_tm(cfg, pref):
    g = cfg.dbatch
    t = (pref // g) * g
    while t > g:
        if cfg.seq % t == 0 and (cfg.dseq * g) % t == 0:
            return t
        t -= g
    return g


def _rowblk(cfg, tm):
    return lambda i: jnp.minimum((i * tm) // cfg.seq, cfg.batch)


def _bdot(a, b):
    return jnp.dot(a.astype(BF16), b.astype(BF16), preferred_element_type=F32)


def _bdot_nt(a, b):
    return lax.dot_general(a.astype(BF16), b.astype(BF16), (((1,), (1,)), ((), ())),
                           preferred_element_type=F32)


def _silu(x):
    return x * jax.nn.sigmoid(x)


def _mm(a, ws, *, grid_m, n_blocks, tm, tn, tk, outs, epilogue, extras=(), row_off=0, name="mm"):
    nk = a.shape[1] // tk
    nw, ne, no = len(ws), len(extras), len(outs)
    in_specs = [pl.BlockSpec((tm, tk), lambda i, j, k: (i + row_off, k))]
    args = [a]
    for w, lead, off in ws:
        in_specs.append(pl.BlockSpec((None,) * len(lead) + (tk, tn),
                                     lambda i, j, k, lead=lead, off=off: lead + (k, j + off)))
        args.append(w)
    for arr, blk, imap in extras:
        in_specs.append(pl.BlockSpec(blk, lambda i, j, k, imap=imap: imap(i, j)))
        args.append(arr)
    out_specs = [pl.BlockSpec(blk, lambda i, j, k, imap=imap: imap(i, j)) for _, _, blk, imap in outs]
    out_shape = [jax.ShapeDtypeStruct(s, d) for s, d, _, _ in outs]

    def body(*refs):
        a_ref = refs[0]
        w_refs = refs[1:1 + nw]
        e_refs = refs[1 + nw:1 + nw + ne]
        o_refs = refs[1 + nw + ne:1 + nw + ne + no]
        acc_refs = refs[1 + nw + ne + no:]
        av = a_ref[...].astype(BF16)
        if nk == 1:
            epilogue([jnp.dot(av, w[...].astype(BF16), preferred_element_type=F32) for w in w_refs],
                     e_refs, o_refs)
            return
        k = pl.program_id(2)

        @pl.when(k == 0)
        def _():
            for acc in acc_refs:
                acc[...] = jnp.zeros(acc.shape, F32)

        for acc, w in zip(acc_refs, w_refs):
            acc[...] += jnp.dot(av, w[...].astype(BF16), preferred_element_type=F32)

        @pl.when(k == nk - 1)
        def _():
            epilogue([acc[...] for acc in acc_refs], e_refs, o_refs)

    scratch = [pltpu.VMEM((tm, tn), F32) for _ in range(nw)] if nk > 1 else []
    res = pl.pallas_call(
        body, grid=(grid_m, n_blocks, nk), in_specs=in_specs, out_specs=out_specs,
        out_shape=out_shape, scratch_shapes=scratch, name=name,
        compiler_params=_cp("parallel", "parallel", "arbitrary"))(*args)
    return res


def _ep_store(accs, e_refs, o_refs):
    o_refs[0][...] = accs[0].astype(o_refs[0].dtype)


def _ep_swiglu(accs, e_refs, o_refs):
    o_refs[0][...] = (_silu(accs[0]) * accs[1]).astype(o_refs[0].dtype)


def _ep_glu(accs, e_refs, o_refs):
    a = accs[0] + e_refs[0][...]
    b = accs[1] + e_refs[1][...]
    o_refs[0][...] = (a * jax.nn.sigmoid(b)).astype(o_refs[0].dtype)


def _ep_resid(coef, has_bias, accs, e_refs, o_refs):
    y = accs[0]
    if has_bias:
        y = y + e_refs[2][...]
    g = e_refs[1][...]
    tm, tn = y.shape
    gy = (y.reshape(tm // g.shape[0], g.shape[0], tn) * g[None]).reshape(tm, tn)
    o_refs[0][...] = e_refs[0][...] + coef * gy


def _plain_mm(cfg, a, w, lead, *, n, col_off=0, out_dtype=F32, tm=1024, tn=512, tk=4096, name="mm"):
    m = a.shape[0]
    tm = _pick(m, tm, 8)
    tn = _pick(n, tn)
    tk = _pick(a.shape[1], tk)
    return _mm(a, [(w, lead, col_off // tn)], grid_m=m // tm, n_blocks=n // tn, tm=tm, tn=tn, tk=tk,
               outs=[((m, n), out_dtype, (tm, tn), lambda i, j: (i, j))], epilogue=_ep_store, name=name)[0]


def _resid_mm(cfg, a, w, lead, x, mod, gate_idx, coef, bias=None, bias_lead=(), tm=1024, tn=512, tk=4096,
              name="resid_mm"):
    m, d = x.shape
    tm = _pick_tm(cfg, tm)
    tn = _pick(d, tn)
    tk = _pick(a.shape[1], tk)
    rb = _rowblk(cfg, tm)
    goff = gate_idx * (d // tn)
    extras = [(x, (tm, tn), lambda i, j: (i, j)),
              (mod, (cfg.dbatch, tn), lambda i, j: (rb(i), goff + j))]
    if bias is not None:
        extras.append((bias, (None,) * len(bias_lead) + (1, tn), lambda i, j: bias_lead + (0, j)))
    return _mm(a, [(w, lead, 0)], grid_m=m // tm, n_blocks=d // tn, tm=tm, tn=tn, tk=tk,
               outs=[((m, d), F32, (tm, tn), lambda i, j: (i, j))], extras=extras,
               epilogue=functools.partial(_ep_resid, coef, bias is not None), name=name)[0]


def _adanorm(cfg, x, norm_g3, gidx, mod, sub):
    m, d = x.shape
    tm = _pick_tm(cfg, 256)
    g = cfg.dbatch
    rb = _rowblk(cfg, tm)

    def body(x_ref, g_ref, sh_ref, sc_ref, o_ref):
        xv = x_ref[...]
        y = xv * lax.rsqrt(jnp.mean(xv * xv, axis=-1, keepdims=True) + EPS) * g_ref[...]
        y3 = y.reshape(tm // g, g, d) * (1.0 + sc_ref[...])[None] + sh_ref[...][None]
        o_ref[...] = y3.reshape(tm, d).astype(BF16)

    return pl.pallas_call(
        body, grid=(m // tm,),
        in_specs=[pl.BlockSpec((tm, d), lambda i: (i, 0)),
                  pl.BlockSpec((None, 1, d), lambda i: (gidx, 0, 0)),
                  pl.BlockSpec((g, d), lambda i: (rb(i), 3 * sub)),
                  pl.BlockSpec((g, d), lambda i: (rb(i), 3 * sub + 1))],
        out_specs=pl.BlockSpec((tm, d), lambda i: (i, 0)),
        out_shape=jax.ShapeDtypeStruct((m, d), BF16),
        compiler_params=_cp("parallel"))(x, norm_g3, mod, mod)


def _layernorm_silu(y, ln_g, ln_b, lead):
    m, c = y.shape
    tm = _pick(m, 256, 8)

    def body(y_ref, g_ref, b_ref, o_ref):
        v = y_ref[...]
        vc = v - jnp.mean(v, axis=-1, keepdims=True)
        n = vc * lax.rsqrt(jnp.mean(vc * vc, axis=-1, keepdims=True) + EPS) * g_ref[...] + b_ref[...]
        o_ref[...] = _silu(n).astype(BF16)

    return pl.pallas_call(
        body, grid=(m // tm,),
        in_specs=[pl.BlockSpec((tm, c), lambda i: (i, 0)),
                  pl.BlockSpec((None, 1, c), lambda i: (lead, 0, 0)),
                  pl.BlockSpec((None, 1, c), lambda i: (lead, 0, 0))],
        out_specs=pl.BlockSpec((tm, c), lambda i: (i, 0)),
        out_shape=jax.ShapeDtypeStruct((m, c), BF16),
        compiler_params=_cp("parallel"))(y, ln_g, ln_b)


def _dwconv(x, init, w, w_lead, *, n_ch, row_off_blocks, batch, rows, tstride, tt, tc, post, out_dtype):
    taps = w.shape[-2]
    halo = (taps - 1) * tstride
    hp = init.shape[1]
    nt = rows // tt
    ncb = n_ch // tc

    def body(x_ref, init_ref, w_ref, o_ref, ext_ref):
        t = pl.program_id(2)

        @pl.when(t == 0)
        def _():
            ext_ref[0:hp, :] = init_ref[...]

        @pl.when(t > 0)
        def _():
            ext_ref[0:hp, :] = ext_ref[tt:tt + hp, :]

        ext_ref[hp:hp + tt, :] = x_ref[...]
        wv = w_ref[...]
        acc = None
        for j in range(taps):
            o = hp - halo + j * tstride
            term = ext_ref[o:o + tt, :] * wv[j:j + 1, :]
            acc = term if acc is None else acc + term
        o_ref[...] = post(acc, pl.program_id(1)).astype(out_dtype)

    return pl.pallas_call(
        body, grid=(batch, ncb, nt),
        in_specs=[pl.BlockSpec((tt, tc), lambda b, c, t: (row_off_blocks + b * nt + t, c)),
                  pl.BlockSpec((None, hp, tc), lambda b, c, t: (b, 0, c)),
                  pl.BlockSpec((None,) * len(w_lead) + (taps, tc), lambda b, c, t: w_lead + (0, c))],
        out_specs=pl.BlockSpec((tt, tc), lambda b, c, t: (b * nt + t, c)),
        out_shape=jax.ShapeDtypeStruct((batch * rows, n_ch), out_dtype),
        scratch_shapes=[pltpu.VMEM((hp + tt, tc), F32)],
        compiler_params=_cp("parallel", "parallel", "arbitrary"))(x, init, w)


def _dwconv_both(cfg, x, w, w_lead, n_ch, init_prompt, init_sample, post, out_dtype, tt_p, tc):
    tt_p = _pick(cfg.seq, tt_p, 8)
    tc = _pick(n_ch, tc)
    ms = cfg.dseq * cfg.dbatch
    yp = _dwconv(x, init_prompt, w, w_lead, n_ch=n_ch, row_off_blocks=0, batch=cfg.batch, rows=cfg.seq,
                 tstride=1, tt=tt_p, tc=tc, post=post, out_dtype=out_dtype)
    assert cfg.m_prompt % ms == 0
    ys = _dwconv(x, init_sample, w, w_lead, n_ch=n_ch, row_off_blocks=cfg.m_prompt // ms, batch=1, rows=ms,
                 tstride=cfg.dbatch, tt=ms, tc=tc, post=post, out_dtype=out_dtype)
    return yp, ys


def _halo_inits(cfg, state, taps, n_ch):
    h = taps - 1
    hp_p = -(-h // 8) * 8
    init_p = jnp.zeros((cfg.batch, hp_p, n_ch), F32)
    init_s = jnp.swapaxes(state.astype(F32), 0, 1).reshape(1, h * cfg.dbatch, n_ch)
    return init_p, init_s


def _delta_rule(qkv, z, ba, s0, s0_lead, gparams, norm_g, *, heads, seg, n_states, steps, hpb, state_out,
                prev_states=None, name="delta_rule"):
    rows = qkv.shape[0]
    r = LANES
    nseg = r // seg
    assert n_states in (1, nseg) and (n_states == 1 or steps == 1) and (n_states == nseg or nseg == 1)
    assert heads % hpb == 0
    groups = rows // (r * steps)
    z_arr, z_off = z
    ns = n_states
    n_layers, layer = state_out
    aliased = prev_states is not None

    def body(*refs):
        q_ref, k_ref, v_ref, z_ref, ba_ref, s0_ref, gp_ref, ng_ref = refs[:8]
        o_ref, sf_ref, s_ref = refs[-3:]
        hblk = pl.program_id(1)
        ci = pl.program_id(2)

        @pl.when(ci == 0)
        def _():
            for hi in range(hpb):
                s_ref[hi * ns:(hi + 1) * ns] = s0_ref[:, hi].astype(F32)

        bav = ba_ref[...]
        gp = gp_ref[...]
        lane = lax.broadcasted_iota(jnp.int32, (r, LANES), 1)
        beta_all = jax.nn.sigmoid(bav)
        xs = bav + gp[1:2, :]
        softplus = jnp.maximum(xs, 0.0) + jnp.log1p(jnp.exp(-jnp.abs(xs)))
        g_all = -jnp.exp(gp[0:1, :]) * softplus
        row = lax.broadcasted_iota(jnp.int32, (r, r), 0)
        col = lax.broadcasted_iota(jnp.int32, (r, r), 1)
        same = (row // seg) == (col // seg)
        incl = jnp.logical_and(same, row >= col)
        strict = jnp.logical_and(same, row > col)
        last = jnp.logical_and(same, col % seg == seg - 1)
        eye = jnp.where(row == col, 1.0, 0.0)
        offs = []
        s = 1
        while s < seg:
            rb = row // s
            offs.append(jnp.logical_and(rb % 2 == 1, (col // s) == rb - 1))
            s *= 2
        rseg = lax.broadcasted_iota(jnp.int32, (r, LANES), 0) // seg
        ng = ng_ref[...]

        hs = range(hpb)
        ls = [slice(hi * LANES, (hi + 1) * LANES) for hi in hs]
        q = [q_ref[:, ls[hi]] for hi in hs]
        k = [k_ref[:, ls[hi]] for hi in hs]
        v = [v_ref[:, ls[hi]] for hi in hs]
        beta = [jnp.sum(jnp.where(lane == hblk * hpb + hi, beta_all, 0.0), axis=-1, keepdims=True) for hi in hs]
        g = [jnp.sum(jnp.where(lane == hblk * hpb + hi + heads, g_all, 0.0), axis=-1, keepdims=True) for hi in hs]
        gc = [jnp.sum(jnp.where(incl, jnp.broadcast_to(g[hi], (r, r)).T, 0.0), axis=-1, keepdims=True)
              for hi in hs]
        gi = [jnp.broadcast_to(gc[hi], (r, r)) for hi in hs]
        gj = [gi[hi].T for hi in hs]
        decay = [jnp.where(incl, jnp.exp(jnp.where(incl, gi[hi] - gj[hi], 0.0)), 0.0) for hi in hs]
        kb = [k[hi] * beta[hi] for hi in hs]
        kk = [_bdot_nt(kb[hi], k[hi]) for hi in hs]
        m_mat = [jnp.where(strict, kk[hi] * decay[hi], 0.0) for hi in hs]
        t_inv = [eye - jnp.where(offs[0], m_mat[hi], 0.0) for hi in hs] if offs else [eye for _ in hs]
        for off in offs[1:]:
            left = [_bdot(t_inv[hi], jnp.where(off, m_mat[hi], 0.0)) for hi in hs]
            corr = [_bdot(left[hi], t_inv[hi]) for hi in hs]
            t_inv = [t_inv[hi] - corr[hi] for hi in hs]
        u = [_bdot(t_inv[hi], v[hi] * beta[hi]) for hi in hs]
        w = [_bdot(t_inv[hi], kb[hi] * jnp.exp(gi[hi])) for hi in hs]
        attn = [_bdot_nt(q[hi], k[hi]) * decay[hi] for hi in hs]
        qg = [q[hi] * jnp.exp(gi[hi]) for hi in hs]
        gl = [jnp.sum(jnp.where(last, gj[hi], 0.0), axis=-1, keepdims=True) for hi in hs]
        kd_t = [(k[hi] * jnp.exp(gl[hi] - gc[hi])).T for hi in hs]
        egl = [jnp.exp(gl[hi]) for hi in hs]
        if ns == 1:
            st = [s_ref[hi] for hi in hs]
            ws = [_bdot(w[hi], st[hi]) for hi in hs]
            dlt = [u[hi] - ws[hi] for hi in hs]
            qs = [_bdot(qg[hi], st[hi]) for hi in hs]
            ad = [_bdot(attn[hi], dlt[hi]) for hi in hs]
            o = [qs[hi] + ad[hi] for hi in hs]
            upd = [_bdot(kd_t[hi], dlt[hi]) for hi in hs]
            for hi in hs:
                s_ref[hi] = st[hi] * egl[hi][r - 1:r, :] + upd[hi]
        else:
            segs = range(nseg)
            wq = [[_bdot(jnp.concatenate([w[hi][si * seg:(si + 1) * seg], qg[hi][si * seg:(si + 1) * seg]],
                                         axis=0), s_ref[hi * ns + si]) for si in segs] for hi in hs]
            dlt = [u[hi] - jnp.concatenate([x[:seg] for x in wq[hi]], axis=0) for hi in hs]
            ad = [_bdot(attn[hi], dlt[hi]) for hi in hs]
            o = [jnp.concatenate([x[seg:] for x in wq[hi]], axis=0) + ad[hi] for hi in hs]
            for hi in hs:
                for si in segs:
                    upd = _bdot(kd_t[hi], jnp.where(rseg == si, dlt[hi], 0.0))
                    s_ref[hi * ns + si] = s_ref[hi * ns + si] * egl[hi][si * seg:si * seg + 1, :] + upd
        for hi in hs:
            on = o[hi] * lax.rsqrt(jnp.mean(o[hi] * o[hi], axis=-1, keepdims=True) + EPS) * ng
            o_ref[:, ls[hi]] = (on * _silu(z_ref[:, ls[hi]])).astype(BF16)

        @pl.when(ci == steps - 1)
        def _():
            for hi in range(hpb):
                sf_ref[:, hi] = s_ref[hi * ns:(hi + 1) * ns]

    hh = heads // hpb
    wl = hpb * LANES
    zb = z_off // hpb
    assert z_off % hpb == 0
    in_specs = [pl.BlockSpec((r, wl), lambda b, h, ci: (b * steps + ci, h)),
                pl.BlockSpec((r, wl), lambda b, h, ci: (b * steps + ci, hh + h)),
                pl.BlockSpec((r, wl), lambda b, h, ci: (b * steps + ci, 2 * hh + h)),
                pl.BlockSpec((r, wl), lambda b, h, ci: (b * steps + ci, zb + h)),
                pl.BlockSpec((r, LANES), lambda b, h, ci: (b * steps + ci, 0)),
                pl.BlockSpec((None,) * len(s0_lead) + (ns, hpb, LANES, LANES),
                             lambda b, h, ci: s0_lead + (b, h, 0, 0)),
                pl.BlockSpec((8, LANES), lambda b, h, ci: (0, 0)),
                pl.BlockSpec((1, LANES), lambda b, h, ci: (0, 0))]
    args = [qkv, qkv, qkv, z_arr, ba, s0, gparams, norm_g]
    aliases = {}
    if aliased:
        in_specs.append(pl.BlockSpec(memory_space=pl.ANY))
        args.append(prev_states)
        aliases = {len(args) - 1: 1}
    return pl.pallas_call(
        body, grid=(groups, heads // hpb, steps), in_specs=in_specs,
        out_specs=[pl.BlockSpec((r, wl), lambda b, h, ci: (b * steps + ci, h)),
                   pl.BlockSpec((None, ns, hpb, LANES, LANES), lambda b, h, ci: (layer, b, h, 0, 0))],
        out_shape=[jax.ShapeDtypeStruct((rows, heads * LANES), BF16),
                   jax.ShapeDtypeStruct((n_layers, groups * ns, heads, LANES, LANES), F32)],
        scratch_shapes=[pltpu.VMEM((hpb * ns, LANES, LANES), F32)],
        input_output_aliases=aliases, name=name,
        compiler_params=_cp("parallel", "parallel", "arbitrary"))(*args)


def _rope_tables(cfg):
    half = cfg.rope // 2
    inv = ROPE_THETA ** (-jnp.arange(0, cfg.rope, 2, dtype=F32) / cfg.rope)
    pos = jnp.concatenate([jnp.tile(jnp.arange(cfg.seq), cfg.batch),
                           jnp.repeat(cfg.n_pages * cfg.page + jnp.arange(cfg.dseq), cfg.dbatch)])
    ang = pos.astype(F32)[:, None] * inv[None, :]
    cos, sin = jnp.cos(ang), jnp.sin(ang)
    zpad = jnp.zeros((cfg.m, LANES - cfg.rope), F32)
    zh = jnp.zeros((cfg.m, half), F32)
    cosf = jnp.concatenate([cos, cos, zpad], axis=1)
    sin_a = jnp.concatenate([-sin, zh, zpad], axis=1)
    sin_b = jnp.concatenate([zh, sin, zpad], axis=1)
    return cosf, sin_a, sin_b


def _rope128(x, cosf, sin_a, sin_b, half):
    return (x * cosf + pltpu.roll(x, LANES - half, 1) * sin_a + pltpu.roll(x, half, 1) * sin_b)


def _mla_cq(cfg, h, w_dq, qa_g3, i):
    m = h.shape[0]
    n = cfg.q_lora
    tm = _pick(m, 1024, 8)
    tk = _pick(cfg.d, 512)

    def ep(accs, e_refs, o_refs):
        y = accs[0]
        y = y * lax.rsqrt(jnp.mean(y * y, axis=-1, keepdims=True) + EPS) * e_refs[0][...]
        o_refs[0][...] = y.astype(BF16)

    return _mm(h, [(w_dq, (i,), 0)], grid_m=m // tm, n_blocks=1, tm=tm, tn=n, tk=tk,
               outs=[((m, n), BF16, (tm, n), lambda i_, j: (i_, 0))],
               extras=[(qa_g3, (None, 1, n), lambda i_, j: (i, 0, 0))], epilogue=ep)[0]


def _mla_q(cfg, cq, w_uq_p, gq, gk, tables):
    m = cq.shape[0]
    hw = 2 * LANES
    tm = _pick(m, 1024, 8)
    qk_dim = cfg.nope + cfg.rope
    half = cfg.rope // 2
    scale = qk_dim ** -0.5

    def ep(accs, e_refs, o_refs):
        cosf, sa, sb, gq_r, gk_r = [r[...] for r in e_refs]
        acc = accs[0]
        rot = _rope128(acc[:, LANES:], cosf, sa, sb, half)
        qf = jnp.concatenate([acc[:, :LANES], rot], axis=-1)
        ssq = jnp.sum(qf * qf, axis=-1, keepdims=True) * (1.0 / qk_dim)
        o_refs[0][...] = (qf * lax.rsqrt(ssq + EPS) * gq_r * (gk_r * scale)).astype(BF16)

    tab = [(t, (tm, LANES), lambda i, j: (i, 0)) for t in tables]
    gains = [(g, (1, hw), lambda i, j: (0, 0)) for g in (gq, gk)]
    return _mm(cq, [(w_uq_p, (), 0)], grid_m=m // tm, n_blocks=cfg.heads, tm=tm, tn=hw, tk=cfg.q_lora,
               outs=[((m, cfg.heads * hw), BF16, (tm, hw), lambda i, j: (i, j))],
               extras=tab + gains, epilogue=ep)[0]


def _mla_ckv(cfg, h, w_dkv_p, kva_g3, i, tables):
    m = h.shape[0]
    kv = cfg.kv_lora
    n = kv + LANES
    tm = _pick(m, 512, 8)
    tk = _pick(cfg.d, 512)
    half = cfg.rope // 2

    def ep(accs, e_refs, o_refs):
        cosf, sa, sb, g = [r[...] for r in e_refs]
        acc = accs[0]
        c = acc[:, :kv]
        o_refs[0][...] = c * lax.rsqrt(jnp.mean(c * c, axis=-1, keepdims=True) + EPS) * g
        o_refs[1][...] = _rope128(acc[:, kv:], cosf, sa, sb, half)

    tab = [(t, (tm, LANES), lambda i_, j: (i_, 0)) for t in tables]
    return _mm(h, [(w_dkv_p, (), 0)], grid_m=m // tm, n_blocks=1, tm=tm, tn=n, tk=tk,
               outs=[((m, kv), F32, (tm, kv), lambda i_, j: (i_, 0)),
                     ((m, LANES), F32, (tm, LANES), lambda i_, j: (i_, 0))],
               extras=tab + [(kva_g3, (None, 1, kv), lambda i_, j: (i, 0, 0))], epilogue=ep)


def _mla_prompt_kv(cfg, ckv, krope_p, w_ukv, i):
    hw = 2 * LANES
    mp = cfg.m_prompt
    tm = _pick(mp, 1024, 8)
    qk_dim = cfg.nope + cfg.rope

    def ep(accs, e_refs, o_refs):
        acc = accs[0]
        kr = e_refs[0][...]
        kn = acc[:, :LANES]
        ssq = (jnp.sum(kn * kn, axis=-1, keepdims=True) + jnp.sum(kr * kr, axis=-1, keepdims=True))
        s = lax.rsqrt(ssq * (1.0 / qk_dim) + EPS)
        o_refs[0][...] = jnp.concatenate([kn * s, kr * s], axis=-1).astype(BF16)
        o_refs[1][...] = acc[:, LANES:].astype(BF16)

    return _mm(ckv, [(w_ukv, (i,), 0)], grid_m=mp // tm, n_blocks=cfg.heads, tm=tm, tn=hw, tk=cfg.kv_lora,
               outs=[((mp, cfg.heads * hw), BF16, (tm, hw), lambda i_, j: (i_, j)),
                     ((mp, cfg.heads * LANES), BF16, (tm, LANES), lambda i_, j: (i_, j))],
               extras=[(krope_p, (tm, LANES), lambda i_, j: (i_, 0))], epilogue=ep)


def _flash_prompt(cfg, q, k, v):
    tq = _pick(cfg.seq, 512, 8)
    nq = cfg.seq // tq
    hw = 2 * LANES

    def body(q_ref, k_ref, v_ref, o_ref, m_ref, l_ref, acc_ref):
        qi = pl.program_id(2)
        kj = pl.program_id(3)

        @pl.when(kj == 0)
        def _():
            m_ref[...] = jnp.full(m_ref.shape, NEG, F32)
            l_ref[...] = jnp.zeros(l_ref.shape, F32)
            acc_ref[...] = jnp.zeros(acc_ref.shape, F32)

        @pl.when(kj <= qi)
        def _():
            s = lax.dot_general(q_ref[...], k_ref[...], (((1,), (1,)), ((), ())), preferred_element_type=F32)
            qpos = qi * tq + lax.broadcasted_iota(jnp.int32, (tq, tq), 0)
            kpos = kj * tq + lax.broadcasted_iota(jnp.int32, (tq, tq), 1)
            s = jnp.where(kpos <= qpos, s, NEG)
            m_prev = m_ref[...]
            m_new = jnp.maximum(m_prev, jnp.max(s, axis=-1, keepdims=True))
            alpha = jnp.exp(m_prev - m_new)
            p = jnp.exp(s - m_new)
            l_ref[...] = alpha * l_ref[...] + jnp.sum(p, axis=-1, keepdims=True)
            acc_ref[...] = alpha * acc_ref[...] + jnp.dot(p.astype(BF16), v_ref[...], preferred_element_type=F32)
            m_ref[...] = m_new

        @pl.when(kj == nq - 1)
        def _():
            o_ref[...] = (acc_ref[...] / l_ref[...]).astype(BF16)

    return pl.pallas_call(
        body, grid=(cfg.batch, cfg.heads, nq, nq),
        in_specs=[pl.BlockSpec((tq, hw), lambda b, h, qi, kj: (b * nq + qi, h)),
                  pl.BlockSpec((tq, hw), lambda b, h, qi, kj: (b * nq + jnp.minimum(kj, qi), h)),
                  pl.BlockSpec((tq, LANES), lambda b, h, qi, kj: (b * nq + jnp.minimum(kj, qi), h))],
        out_specs=pl.BlockSpec((tq, LANES), lambda b, h, qi, kj: (b * nq + qi, h)),
        out_shape=jax.ShapeDtypeStruct((cfg.m_prompt, cfg.heads * LANES), BF16),
        scratch_shapes=[pltpu.VMEM((tq, 1), F32), pltpu.VMEM((tq, 1), F32), pltpu.VMEM((tq, LANES), F32)],
        compiler_params=_cp("parallel", "parallel", "parallel", "arbitrary"))(q, k, v)


def _head_mm(a, w, *, a_block_of_head, kdim, n, out_dtype):
    rows = a.shape[0]
    heads = w.shape[0]

    def body(a_ref, w_ref, o_ref):
        o_ref[...] = jnp.dot(a_ref[...].astype(BF16), w_ref[...], preferred_element_type=F32).astype(out_dtype)

    return pl.pallas_call(
        body, grid=(heads,),
        in_specs=[pl.BlockSpec((rows, kdim), lambda h: (0, a_block_of_head(h))),
                  pl.BlockSpec((None, kdim, n), lambda h: (h, 0, 0))],
        out_specs=pl.BlockSpec((rows, n), lambda h: (0, h)),
        out_shape=jax.ShapeDtypeStruct((rows, heads * n), out_dtype),
        compiler_params=_cp("parallel"))(a, w)


def _decode_attention(cfg, page_table, cache_ckv, cache_krope_t, layer, qabs, qr, c_new, kr_new_t, w_uk_t):
    db, hq, kv = qabs.shape
    tq = cfg.dseq
    heads = cfg.heads
    npg = cfg.n_pages
    assert npg % 2 == 0 and hq == heads * tq and tq == 8
    npair = npg // 2
    page = cfg.page
    qk_dim = cfg.nope + cfg.rope
    hb_rows = min(heads, 8) * cfg.nope

    def body(pt_ref, c0_ref, c1_ref, r0_ref, r1_ref, qa_ref, qr_ref, cn_ref, rn_ref, wt_ref,
             o_ref, m_ref, l_ref, acc_ref, s_ref):
        p = pl.program_id(1)

        @pl.when(p == 0)
        def _():
            m_ref[...] = jnp.full(m_ref.shape, NEG, F32)
            l_ref[...] = jnp.zeros(l_ref.shape, F32)
            acc_ref[...] = jnp.zeros(acc_ref.shape, F32)

        def process(c, kr_t, new_rows):
            nkeys = c.shape[0]
            cb = c.astype(BF16)
            ssq_r = jnp.sum(kr_t * kr_t, axis=0, keepdims=True)
            s_all = _bdot_nt(qa_ref[...], cb) + _bdot(qr_ref[...], kr_t)
            for hb in range(heads * cfg.nope // hb_rows):
                kt = _bdot_nt(wt_ref[hb * hb_rows:(hb + 1) * hb_rows, :], cb)
                for hh in range(hb_rows // cfg.nope):
                    hd = hb * (hb_rows // cfg.nope) + hh
                    sl = kt[hh * cfg.nope:(hh + 1) * cfg.nope, :]
                    ssq = jnp.sum(sl * sl, axis=0, keepdims=True) + ssq_r
                    ksc = lax.rsqrt(ssq * (1.0 / qk_dim) + EPS)
                    s_ref[hd * tq:(hd + 1) * tq, 0:nkeys] = s_all[hd * tq:(hd + 1) * tq, :] * ksc
            s = s_ref[:, 0:nkeys]
            if new_rows:
                qrow = lax.broadcasted_iota(jnp.int32, (hq, nkeys), 0) % tq
                kcol = lax.broadcasted_iota(jnp.int32, (hq, nkeys), 1)
                s = jnp.where(kcol <= qrow, s, NEG)
            m_prev = m_ref[...]
            m_new = jnp.maximum(m_prev, jnp.max(s, axis=-1, keepdims=True))
            alpha = jnp.exp(m_prev - m_new)
            pr = jnp.exp(s - m_new)
            l_ref[...] = alpha * l_ref[...] + jnp.sum(pr, axis=-1, keepdims=True)
            acc_ref[...] = alpha * acc_ref[...] + jnp.dot(pr.astype(BF16), cb, preferred_element_type=F32)
            m_ref[...] = m_new

        @pl.when(p < npair)
        def _():
            process(jnp.concatenate([c0_ref[...], c1_ref[...]], axis=0),
                    jnp.concatenate([r0_ref[...], r1_ref[...]], axis=1), False)

        @pl.when(p == npair)
        def _():
            process(cn_ref[...], rn_ref[...], True)
            o_ref[...] = acc_ref[...] / l_ref[...]

    def pg(which):
        def imap(b, p, pt):
            pp = jnp.minimum(p, npair - 1)
            return (layer, pt[b * npg + 2 * pp + which], 0, 0)
        return imap

    grid_spec = pltpu.PrefetchScalarGridSpec(
        num_scalar_prefetch=1, grid=(db, npair + 1),
        in_specs=[pl.BlockSpec((None, None, page, kv), pg(0)),
                  pl.BlockSpec((None, None, page, kv), pg(1)),
                  pl.BlockSpec((None, None, cfg.rope, page), pg(0)),
                  pl.BlockSpec((None, None, cfg.rope, page), pg(1)),
                  pl.BlockSpec((None, hq, kv), lambda b, p, pt: (b, 0, 0)),
                  pl.BlockSpec((None, hq, cfg.rope), lambda b, p, pt: (b, 0, 0)),
                  pl.BlockSpec((None, LANES, kv), lambda b, p, pt: (b, 0, 0)),
                  pl.BlockSpec((None, cfg.rope, LANES), lambda b, p, pt: (b, 0, 0)),
                  pl.BlockSpec((heads * cfg.nope, kv), lambda b, p, pt: (0, 0))],
        out_specs=pl.BlockSpec((None, hq, kv), lambda b, p, pt: (b, 0, 0)),
        scratch_shapes=[pltpu.VMEM((hq, 1), F32), pltpu.VMEM((hq, 1), F32), pltpu.VMEM((hq, kv), F32),
                        pltpu.VMEM((hq, 2 * page), F32)])
    return pl.pallas_call(
        body, grid_spec=grid_spec, out_shape=jax.ShapeDtypeStruct((db, hq, kv), F32),
        name="decode_attention", compiler_params=_cp("parallel", "arbitrary"))(
            page_table.reshape(-1), cache_ckv, cache_ckv, cache_krope_t, cache_krope_t,
            qabs, qr, c_new, kr_new_t, w_uk_t)


def _to_batch_major(cfg, rows):
    return jnp.swapaxes(rows.reshape(cfg.dseq, cfg.dbatch, rows.shape[-1]), 0, 1)


def _to_time_major(cfg, arr):
    return jnp.swapaxes(arr, 0, 1).reshape(cfg.dseq * cfg.dbatch, arr.shape[-1])


def _last_rows_prompt(cfg, rows, n_last, n_ch):
    return jnp.stack([lax.slice(rows, ((b + 1) * cfg.seq - n_last, 0), ((b + 1) * cfg.seq, n_ch))
                      for b in range(cfg.batch)])


def _last_rows_sample(cfg, rows, state, n_last, n_ch):
    mp, g, t = cfg.m_prompt, cfg.dbatch, cfg.dseq
    keep = max(n_last - t, 0)
    new = lax.slice(rows, (mp + max(t - n_last, 0) * g, 0), (mp + t * g, n_ch))
    new = jnp.swapaxes(new.reshape(-1, g, n_ch), 0, 1)
    if keep == 0:
        return new
    return jnp.concatenate([state[:, n_last - keep:].astype(F32), new], axis=1)


def _ffn(cfg, x, mod, norm_g3, l, sub, which, wg, wu, wd):
    h = _adanorm(cfg, x, norm_g3, 3 * l + sub, mod, sub)
    m, d = x.shape
    dff = wg.shape[-1]
    tm = _pick(m, 1024, 8)
    tn = _pick(dff, 256)
    tk = _pick(d, 4096)
    act = _mm(h, [(wg, (l, which), 0), (wu, (l, which), 0)], grid_m=m // tm, n_blocks=dff // tn,
              tm=tm, tn=tn, tk=tk, outs=[((m, dff), BF16, (tm, tn), lambda i, j: (i, j))],
              epilogue=_ep_swiglu, name="ffn_up")[0]
    return _resid_mm(cfg, act, wd, (l, which), x, mod, 3 * sub + 2, 0.5, name="ffn_down")


def _deltanet_layer(cfg, h, i, state_delta, state_delta_conv, dn_w_in, dn_conv_w, dn_a_log, dn_dt_bias,
                    dn_norm_g, prompt_chunk, prev_p, prev_s):
    heads, dk = cfg.heads, cfg.dk
    d_a = heads * dk
    n_qkvz = 4 * d_a
    mp = cfg.m_prompt
    proj = _plain_mm(cfg, h, dn_w_in, (i,), n=n_qkvz)
    w_ba = jnp.pad(dn_w_in[i, :, n_qkvz:], ((0, 0), (0, LANES - 2 * heads)))
    ba = _plain_mm(cfg, h, w_ba, (), n=LANES, tn=LANES)

    nq = d_a
    tc = _pick(3 * d_a, 512)

    def post(y, cb):
        y = _silu(y)
        is_qk = cb < (2 * nq) // tc
        is_q = cb < nq // tc
        outs = []
        for s in range(tc // dk):
            ys = y[:, s * dk:(s + 1) * dk]
            ssq = jnp.sum(ys * ys, axis=-1, keepdims=True)
            sc = jnp.where(is_qk, lax.rsqrt(ssq + EPS), 1.0) * jnp.where(is_q, dk ** -0.5, 1.0)
            outs.append(ys * sc)
        return jnp.concatenate(outs, axis=-1)

    taps = dn_conv_w.shape[1]
    init_p, init_s = _halo_inits(cfg, state_delta_conv[i], taps, 3 * d_a)
    qkv_p, qkv_s = _dwconv_both(cfg, proj, dn_conv_w, (i,), 3 * d_a, init_p, init_s, post, F32, 256, tc)

    gparams = jnp.zeros((8, LANES), F32)
    gparams = gparams.at[0, heads:2 * heads].set(dn_a_log[i].astype(F32))
    gparams = gparams.at[1, heads:2 * heads].set(dn_dt_bias[i].astype(F32))
    ng = dn_norm_g[i].astype(F32).reshape(1, dk)

    s0_p = jnp.zeros((cfg.batch, heads, dk, dk), F32)
    n_layers = state_delta.shape[0]
    o_p, s_p = _delta_rule(qkv_p, (proj, 3 * heads), ba, s0_p, (), gparams, ng,
                           heads=heads, seg=prompt_chunk, n_states=1, steps=cfg.seq // LANES,
                           hpb=_pick(heads, 8, 1), state_out=(n_layers, i), prev_states=prev_p,
                           name="delta_rule_prompt")
    ms = cfg.dseq * cfg.dbatch
    o_s, s_s = _delta_rule(_to_batch_major(cfg, qkv_s).reshape(ms, 3 * d_a),
                           (_to_batch_major(cfg, proj[mp:, 3 * d_a:]).reshape(ms, d_a), 0),
                           _to_batch_major(cfg, ba[mp:]).reshape(ms, LANES), state_delta, (i,), gparams, ng,
                           heads=heads, seg=cfg.dseq, n_states=LANES // cfg.dseq, steps=1,
                           hpb=_pick(heads, 2, 1), state_out=(n_layers, i), prev_states=prev_s,
                           name="delta_rule_sample")
    o_all = jnp.concatenate([o_p, _to_time_major(cfg, o_s.reshape(cfg.dbatch, cfg.dseq, d_a))], axis=0)
    new_conv_p = _last_rows_prompt(cfg, proj, taps - 1, 3 * d_a)
    new_conv_s = _last_rows_sample(cfg, proj, state_delta_conv[i], taps - 1, 3 * d_a)
    return o_all, s_p, s_s, new_conv_p, new_conv_s


def _conformer_layer(cfg, h, i, state_conv, cv_w_pw1, cv_b_pw1, cv_w_dw, cv_b_dw, cv_ln_g, cv_ln_b):
    m, d = h.shape
    dc = cv_w_dw.shape[-1]
    tm = _pick(m, 1024, 8)
    tn = _pick(dc, 256)
    tk = _pick(d, 4096)
    b3 = cv_b_pw1.reshape(cv_b_pw1.shape[0], 1, 2 * dc)
    u = _mm(h, [(cv_w_pw1, (i,), 0), (cv_w_pw1, (i,), dc // tn)], grid_m=m // tm, n_blocks=dc // tn,
            tm=tm, tn=tn, tk=tk, outs=[((m, dc), F32, (tm, tn), lambda i_, j: (i_, j))],
            extras=[(b3, (None, 1, tn), lambda i_, j: (i, 0, j)),
                    (b3, (None, 1, tn), lambda i_, j: (i, 0, dc // tn + j))],
            epilogue=_ep_glu, name="conformer_glu")[0]
    taps = cv_w_dw.shape[1]
    init_p, init_s = _halo_inits(cfg, state_conv[i], taps, dc)
    yp, ys = _dwconv_both(cfg, u, cv_w_dw, (i,), dc, init_p, init_s, lambda y, cb: y, F32, 128, 256)
    y = jnp.concatenate([yp, ys], axis=0) + cv_b_dw[i][None, :]
    hn = _layernorm_silu(y, cv_ln_g.reshape(-1, 1, dc), cv_ln_b.reshape(-1, 1, dc), i)
    new_p = _last_rows_prompt(cfg, u, taps - 1, dc)
    new_s = _last_rows_sample(cfg, u, state_conv[i], taps - 1, dc)
    return hn, new_p, new_s


def _mla_layer(cfg, h, i, cache_ckv, cache_krope, page_table, tables, m_w_dq, m_qa_g, m_w_uq, m_w_dkv,
               m_kva_g, m_w_ukv, m_q_norm_g, m_k_norm_g):
    heads, nope, rope, kv = cfg.heads, cfg.nope, cfg.rope, cfg.kv_lora
    qk_dim = nope + rope
    hw = 2 * LANES
    mp = cfg.m_prompt
    half = rope // 2

    def tied(g):
        g = g.astype(F32)
        return jnp.concatenate([g[:nope], g[nope:], g[nope:], jnp.zeros((hw - qk_dim,), F32)]).reshape(1, hw)

    w_uq_p = jnp.pad(m_w_uq[i].reshape(cfg.q_lora, heads, qk_dim),
                     ((0, 0), (0, 0), (0, hw - qk_dim))).reshape(cfg.q_lora, heads * hw)
    w_dkv_p = jnp.pad(m_w_dkv[i], ((0, 0), (0, LANES - rope)))
    cq = _mla_cq(cfg, h, m_w_dq, m_qa_g.reshape(-1, 1, cfg.q_lora), i)
    q = _mla_q(cfg, cq, w_uq_p, tied(m_q_norm_g[i]), tied(m_k_norm_g[i]), tables)
    ckv, krope_p = _mla_ckv(cfg, h, w_dkv_p, m_kva_g.reshape(-1, 1, kv), i, tables)

    k_p, v_p = _mla_prompt_kv(cfg, ckv, krope_p, m_w_ukv, i)
    att_p = _flash_prompt(cfg, q, k_p, v_p)

    w3 = m_w_ukv[i].reshape(kv, heads, nope + LANES)
    w_uk_h = jnp.transpose(w3[:, :, :nope], (1, 2, 0)).astype(BF16)
    w_uv_h = jnp.transpose(w3[:, :, nope:], (1, 0, 2)).astype(BF16)
    q_s = q[mp:]
    ms = q_s.shape[0]
    qabs = _head_mm(q_s, w_uk_h, a_block_of_head=lambda hd: 2 * hd, kdim=nope, n=kv, out_dtype=BF16)
    qabs = jnp.transpose(qabs.reshape(cfg.dseq, cfg.dbatch, heads, kv), (1, 2, 0, 3)).reshape(
        cfg.dbatch, heads * cfg.dseq, kv)
    qr = q_s.reshape(cfg.dseq, cfg.dbatch, heads, hw)[..., nope:qk_dim]
    qr = jnp.transpose(qr, (1, 2, 0, 3)).reshape(cfg.dbatch, heads * cfg.dseq, rope)
    ckv_s = _to_batch_major(cfg, ckv[mp:])
    kr_s = _to_batch_major(cfg, krope_p[mp:, :rope])
    c_new = jnp.pad(ckv_s, ((0, 0), (0, LANES - cfg.dseq), (0, 0)))
    kr_new_t = jnp.pad(jnp.swapaxes(kr_s, 1, 2), ((0, 0), (0, 0), (0, LANES - cfg.dseq)))
    o_lat = _decode_attention(cfg, page_table, cache_ckv, jnp.swapaxes(cache_krope, 2, 3), i, qabs, qr, c_new,
                              kr_new_t,
                              w_uk_h.reshape(heads * nope, kv))
    o_lat = jnp.transpose(o_lat.reshape(cfg.dbatch, heads, cfg.dseq, kv), (2, 0, 1, 3)).reshape(ms, heads * kv)
    att_s = _head_mm(o_lat, w_uv_h, a_block_of_head=lambda hd: hd, kdim=kv, n=LANES, out_dtype=BF16)
    att = jnp.concatenate([att_p, att_s], axis=0)
    ckv_p_out = ckv[:mp].reshape(cfg.batch, cfg.seq, kv)
    kr_p_out = krope_p[:mp, :rope].reshape(cfg.batch, cfg.seq, rope)
    return att, ckv_p_out, kr_p_out, ckv_s, kr_s


def kernel(x_prompt, x_sample, cache_ckv, cache_krope, state_delta, state_delta_conv, state_conv, page_table,
           c_prompt, c_sample, ada_w, ada_b, norm_g, ffn_wg, ffn_wu, ffn_wd, dn_w_in, dn_conv_w, dn_a_log,
           dn_dt_bias, dn_norm_g, dn_w_out, cv_w_pw1, cv_b_pw1, cv_w_dw, cv_b_dw, cv_ln_g, cv_ln_b, cv_w_pw2,
           cv_b_pw2, m_w_dq, m_qa_g, m_w_uq, m_w_dkv, m_kva_g, m_w_ukv, m_q_norm_g, m_k_norm_g, m_w_o):
    batch, seq, d = x_prompt.shape
    dbatch, dseq, _ = x_sample.shape
    depth = ada_w.shape[0]
    rope = cache_krope.shape[-1]
    kv_lora = cache_ckv.shape[-1]
    heads = dn_a_log.shape[-1]
    nope = m_q_norm_g.shape[-1] - rope // 2
    cfg = Cfg(batch=batch, seq=seq, dbatch=dbatch, dseq=dseq, d=d, m_prompt=batch * seq,
              m=batch * seq + dbatch * dseq, n_pages=page_table.shape[1], page=cache_ckv.shape[2],
              heads=heads, dk=state_delta.shape[-2], nope=nope, rope=rope, kv_lora=kv_lora,
              q_lora=m_w_dq.shape[-1])
    assert dbatch % 8 == 0 and seq % dbatch == 0 and cfg.dk == LANES and nope == LANES

    x = jnp.concatenate([x_prompt.reshape(batch * seq, d).astype(F32),
                         jnp.swapaxes(x_sample.astype(F32), 0, 1).reshape(dseq * dbatch, d)], axis=0)
    c_all = jnp.concatenate([jnp.repeat(c_prompt, dbatch, axis=0), c_sample], axis=0).astype(F32)
    norm_g3 = norm_g.reshape(depth * 3, 1, d).astype(F32)
    ada_b3 = ada_b.reshape(depth, 1, N_ADA * d)
    tables = _rope_tables(cfg)
    mrows = c_all.shape[0]
    tn_ada = _pick(N_ADA * d, 512)
    tk_ada = _pick(d, 4096)

    def silu_body(c_ref, o_ref):
        o_ref[...] = _silu(c_ref[...]).astype(BF16)

    c_act = pl.pallas_call(silu_body, out_shape=jax.ShapeDtypeStruct(c_all.shape, BF16), name="ada_silu")(c_all)

    def ep_bias(accs, e_refs, o_refs):
        o_refs[0][...] = accs[0] + e_refs[0][...]

    outs = {k: [] for k in ("ckv_p", "kr_p", "ckv_s", "kr_s", "dnc_p", "dnc_s", "cv_p", "cv_s")}
    dn_p = dn_s = None
    for l in range(depth):
        mod = _mm(c_act, [(ada_w, (l,), 0)], grid_m=1, n_blocks=N_ADA * d // tn_ada, tm=mrows, tn=tn_ada,
                  tk=tk_ada, outs=[((mrows, N_ADA * d), F32, (mrows, tn_ada), lambda i, j: (0, j))],
                  extras=[(ada_b3, (None, 1, tn_ada), lambda i, j, l=l: (l, 0, j))], epilogue=ep_bias,
                  name="ada")[0]
        x = _ffn(cfg, x, mod, norm_g3, l, 0, 0, ffn_wg, ffn_wu, ffn_wd)
        h = _adanorm(cfg, x, norm_g3, 3 * l + 1, mod, 1)
        kind, i = l % 3, l // 3
        if kind == 0:
            o, dn_p, dn_s, nc_p, nc_s = _deltanet_layer(cfg, h, i, state_delta, state_delta_conv, dn_w_in,
                                                        dn_conv_w, dn_a_log, dn_dt_bias, dn_norm_g,
                                                        min(LANES, seq), dn_p, dn_s)
            x = _resid_mm(cfg, o, dn_w_out, (i,), x, mod, 5, 1.0)
            outs["dnc_p"].append(nc_p)
            outs["dnc_s"].append(nc_s)
        elif kind == 1:
            hn, nb_p, nb_s = _conformer_layer(cfg, h, i, state_conv, cv_w_pw1, cv_b_pw1, cv_w_dw, cv_b_dw,
                                              cv_ln_g, cv_ln_b)
            x = _resid_mm(cfg, hn, cv_w_pw2, (i,), x, mod, 5, 1.0,
                          bias=cv_b_pw2.reshape(-1, 1, d), bias_lead=(i,))
            outs["cv_p"].append(nb_p)
            outs["cv_s"].append(nb_s)
        else:
            att, ckv_p, kr_p, ckv_s, kr_s = _mla_layer(cfg, h, i, cache_ckv, cache_krope, page_table, tables,
                                                       m_w_dq, m_qa_g, m_w_uq, m_w_dkv, m_kva_g, m_w_ukv,
                                                       m_q_norm_g, m_k_norm_g)
            x = _resid_mm(cfg, att, m_w_o, (i,), x, mod, 5, 1.0)
            outs["ckv_p"].append(ckv_p)
            outs["kr_p"].append(kr_p)
            outs["ckv_s"].append(ckv_s)
            outs["kr_s"].append(kr_s)
        x = _ffn(cfg, x, mod, norm_g3, l, 2, 1, ffn_wg, ffn_wu, ffn_wd)

    mp = cfg.m_prompt
    y_prompt = x[:mp].reshape(batch, seq, d)
    y_sample = jnp.swapaxes(x[mp:].reshape(dseq, dbatch, d), 0, 1)
    st = jnp.stack
    return (y_prompt, y_sample, st(outs["ckv_p"]), st(outs["kr_p"]), st(outs["ckv_s"]), st(outs["kr_s"]),
            dn_p, dn_s, st(outs["dnc_p"]), st(outs["dnc_s"]),
            st(outs["cv_p"]), st(outs["cv_s"]))
```

```python
import collections
import functools

import jax
import jax.numpy as jnp
from jax import lax
from jax.experimental import pallas as pl
from jax.experimental.pallas import tpu as pltpu

F32 = jnp.float32
BF16 = jnp.bfloat16
EPS = 1e-6
N_ADA = 9
ROPE_THETA = 10000.0
LANES = 128
VMEM_LIMIT = 52 * 1024 * 1024
NEG = -1e30

Cfg = collections.namedtuple(
    "Cfg", "batch seq dbatch dseq d m_prompt m n_pages page heads dk nope rope kv_lora q_lora")


def _cp(*sem):
    return pltpu.CompilerParams(dimension_semantics=sem, vmem_limit_bytes=VMEM_LIMIT)


def _pick(dim, pref, mult=LANES):
    if dim <= pref:
        return dim
    t = (pref // mult) * mult
    while t >= mult:
        if dim % t == 0:
            return t
        t -= mult
    return dim


def _pick_tm(cfg, pref):
    g = cfg.dbatch
    t = (pref // g) * g
    while t > g:
        if cfg.seq % t == 0 and (cfg.dseq * g) % t == 0:
            return t
        t -= g
    return g


def _rowblk(cfg, tm):
    return lambda i: jnp.minimum((i * tm) // cfg.seq, cfg.batch)


def _bdot(a, b):
    return jnp.dot(a.astype(BF16), b.astype(BF16), preferred_element_type=F32)


def _bdot_nt(a, b):
    return lax.dot_general(a.astype(BF16), b.astype(BF16), (((1,), (1,)), ((), ())),
                           preferred_element_type=F32)


def _silu(x):
    return x * jax.nn.sigmoid(x)


def _mm(a, ws, *, grid_m, n_blocks, tm, tn, tk, outs, epilogue, extras=(), row_off=0, w_nt=False, name="mm"):
    nk = a.shape[1] // tk
    nw, ne, no = len(ws), len(extras), len(outs)
    in_specs = [pl.BlockSpec((tm, tk), lambda i, j, k: (i + row_off, k))]
    args = [a]
    for w, lead, off in ws:
        if w_nt:
            in_specs.append(pl.BlockSpec((None,) * len(lead) + (tn, tk),
                                         lambda i, j, k, lead=lead, off=off: lead + (j + off, k)))
        else:
            in_specs.append(pl.BlockSpec((None,) * len(lead) + (tk, tn),
                                         lambda i, j, k, lead=lead, off=off: lead + (k, j + off)))
        args.append(w)

    def wdot(av, w):
        dims = (((1,), (1,)), ((), ())) if w_nt else (((1,), (0,)), ((), ()))
        return lax.dot_general(av[...].astype(BF16), w[...].astype(BF16), dims, preferred_element_type=F32)
    for arr, blk, imap in extras:
        in_specs.append(pl.BlockSpec(blk, lambda i, j, k, imap=imap: imap(i, j)))
        args.append(arr)
    out_specs = [pl.BlockSpec(blk, lambda i, j, k, imap=imap: imap(i, j)) for _, _, blk, imap in outs]
    out_shape = [jax.ShapeDtypeStruct(s, d) for s, d, _, _ in outs]

    def body(*refs):
        a_ref = refs[0]
        w_refs = refs[1:1 + nw]
        e_refs = refs[1 + nw:1 + nw + ne]
        o_refs = refs[1 + nw + ne:1 + nw + ne + no]
        acc_refs = refs[1 + nw + ne + no:]
        av = a_ref
        if nk == 1:
            epilogue([wdot(av, w) for w in w_refs], e_refs, o_refs)
            return
        k = pl.program_id(2)

        @pl.when(k == 0)
        def _():
            for acc, w in zip(acc_refs, w_refs):
                acc[...] = wdot(av, w)

        if nk > 2:
            @pl.when(jnp.logical_and(k > 0, k < nk - 1))
            def _():
                for acc, w in zip(acc_refs, w_refs):
                    acc[...] += wdot(av, w)

        @pl.when(k == nk - 1)
        def _():
            epilogue([acc[...] + wdot(av, w) for acc, w in zip(acc_refs, w_refs)], e_refs, o_refs)

    scratch = [pltpu.VMEM((tm, tn), F32) for _ in range(nw)] if nk > 1 else []
    res = pl.pallas_call(
        body, grid=(grid_m, n_blocks, nk), in_specs=in_specs, out_specs=out_specs,
        out_shape=out_shape, scratch_shapes=scratch, name=name,
        compiler_params=_cp("parallel", "parallel", "arbitrary"))(*args)
    return res


def _ep_store(accs, e_refs, o_refs):
    o_refs[0][...] = accs[0].astype(o_refs[0].dtype)


def _ep_swiglu(accs, e_refs, o_refs):
    o_refs[0][...] = (_silu(accs[0]) * accs[1]).astype(o_refs[0].dtype)


def _ep_glu(accs, e_refs, o_refs):
    a = accs[0] + e_refs[0][...]
    b = accs[1] + e_refs[1][...]
    o_refs[0][...] = (a * jax.nn.sigmoid(b)).astype(o_refs[0].dtype)


def _ep_resid(coef, has_bias, accs, e_refs, o_refs):
    y = accs[0]
    if has_bias:
        y = y + e_refs[2][...]
    g = e_refs[1][...]
    tm, tn = y.shape
    gy = (y.reshape(tm // g.shape[0], g.shape[0], tn) * g[None]).reshape(tm, tn)
    o_refs[0][...] = e_refs[0][...] + coef * gy


def _plain_mm(cfg, a, w, lead, *, n, col_off=0, out_dtype=F32, tm=1024, tn=512, tk=4096, w_nt=False,
              name="mm"):
    m = a.shape[0]
    tm = _pick(m, tm, 8)
    tn = _pick(n, tn)
    tk = _pick(a.shape[1], tk)
    return _mm(a, [(w, lead, col_off // tn)], grid_m=m // tm, n_blocks=n // tn, tm=tm, tn=tn, tk=tk,
               outs=[((m, n), out_dtype, (tm, tn), lambda i, j: (i, j))], epilogue=_ep_store, w_nt=w_nt,
               name=name)[0]


def _resid_mm(cfg, a, w, lead, x, mod, gate_idx, coef, bias=None, bias_lead=(), tm=1024, tn=512, tk=4096,
              name="resid_mm"):
    m, d = x.shape
    tm = _pick_tm(cfg, tm)
    tn = _pick(d, tn)
    tk = _pick(a.shape[1], tk)
    rb = _rowblk(cfg, tm)
    goff = gate_idx * (d // tn)
    extras = [(x, (tm, tn), lambda i, j: (i, j)),
              (mod, (cfg.dbatch, tn), lambda i, j: (rb(i), goff + j))]
    if bias is not None:
        extras.append((bias, (None,) * len(bias_lead) + (1, tn), lambda i, j: bias_lead + (0, j)))
    return _mm(a, [(w, lead, 0)], grid_m=m // tm, n_blocks=d // tn, tm=tm, tn=tn, tk=tk,
               outs=[((m, d), F32, (tm, tn), lambda i, j: (i, j))], extras=extras,
               epilogue=functools.partial(_ep_resid, coef, bias is not None), name=name)[0]


def _adanorm(cfg, x, norm_g3, gidx, mod, sub):
    m, d = x.shape
    tm = _pick_tm(cfg, 256)
    g = cfg.dbatch
    rb = _rowblk(cfg, tm)

    def body(x_ref, g_ref, sh_ref, sc_ref, o_ref):
        chunks = [slice(c * LANES, (c + 1) * LANES) for c in range(d // LANES)]
        acc = jnp.zeros((tm, LANES), F32)
        for sl in chunks:
            xc = x_ref[:, sl]
            acc = acc + xc * xc
        inv = lax.rsqrt(jnp.sum(acc, axis=-1, keepdims=True) * (1.0 / d) + EPS)
        for sl in chunks:
            y = x_ref[:, sl] * inv * g_ref[:, sl]
            y3 = y.reshape(tm // g, g, LANES) * (1.0 + sc_ref[:, sl])[None] + sh_ref[:, sl][None]
            o_ref[:, sl] = y3.reshape(tm, LANES).astype(BF16)

    return pl.pallas_call(
        body, grid=(m // tm,),
        in_specs=[pl.BlockSpec((tm, d), lambda i: (i, 0)),
                  pl.BlockSpec((None, 1, d), lambda i: (gidx, 0, 0)),
                  pl.BlockSpec((g, d), lambda i: (rb(i), 3 * sub)),
                  pl.BlockSpec((g, d), lambda i: (rb(i), 3 * sub + 1))],
        out_specs=pl.BlockSpec((tm, d), lambda i: (i, 0)),
        out_shape=jax.ShapeDtypeStruct((m, d), BF16),
        compiler_params=_cp("parallel"))(x, norm_g3, mod, mod)


def _layernorm_silu(y, ln_g, ln_b, lead):
    m, c = y.shape
    tm = _pick(m, 256, 8)

    def body(y_ref, g_ref, b_ref, o_ref):
        v = y_ref[...]
        vc = v - jnp.mean(v, axis=-1, keepdims=True)
        n = vc * lax.rsqrt(jnp.mean(vc * vc, axis=-1, keepdims=True) + EPS) * g_ref[...] + b_ref[...]
        o_ref[...] = _silu(n).astype(BF16)

    return pl.pallas_call(
        body, grid=(m // tm,),
        in_specs=[pl.BlockSpec((tm, c), lambda i: (i, 0)),
                  pl.BlockSpec((None, 1, c), lambda i: (lead, 0, 0)),
                  pl.BlockSpec((None, 1, c), lambda i: (lead, 0, 0))],
        out_specs=pl.BlockSpec((tm, c), lambda i: (i, 0)),
        out_shape=jax.ShapeDtypeStruct((m, c), BF16),
        compiler_params=_cp("parallel"))(y, ln_g, ln_b)


def _dwconv(x, init, w, w_lead, *, n_ch, row_off_blocks, batch, rows, tstride, tt, tc, post, out_dtype):
    taps = w.shape[-2]
    halo = (taps - 1) * tstride
    hp = init.shape[1]
    nt = rows // tt
    ncb = n_ch // tc

    def body(x_ref, init_ref, w_ref, o_ref, ext_ref):
        t = pl.program_id(2)

        @pl.when(t == 0)
        def _():
            ext_ref[0:hp, :] = init_ref[...]

        @pl.when(t > 0)
        def _():
            ext_ref[0:hp, :] = ext_ref[tt:tt + hp, :]

        ext_ref[hp:hp + tt, :] = x_ref[...]
        wv = w_ref[...]
        acc = None
        if tstride == 1 and taps > 8:
            for r_ in range(8):
                n_m = (taps - 1 - r_) // 8 + 1
                o = hp - halo + r_
                win = ext_ref[o:o + tt + 8 * (n_m - 1), :]
                for m_ in range(n_m):
                    term = win[8 * m_:8 * m_ + tt, :] * wv[r_ + 8 * m_:r_ + 8 * m_ + 1, :]
                    acc = term if acc is None else acc + term
        else:
            for j in range(taps):
                o = hp - halo + j * tstride
                term = ext_ref[o:o + tt, :] * wv[j:j + 1, :]
                acc = term if acc is None else acc + term
        o_ref[...] = post(acc, pl.program_id(1)).astype(out_dtype)

    return pl.pallas_call(
        body, grid=(batch, ncb, nt),
        in_specs=[pl.BlockSpec((tt, tc), lambda b, c, t: (row_off_blocks + b * nt + t, c)),
                  pl.BlockSpec((None, hp, tc), lambda b, c, t: (b, 0, c)),
                  pl.BlockSpec((None,) * len(w_lead) + (taps, tc), lambda b, c, t: w_lead + (0, c))],
        out_specs=pl.BlockSpec((tt, tc), lambda b, c, t: (b * nt + t, c)),
        out_shape=jax.ShapeDtypeStruct((batch * rows, n_ch), out_dtype),
        scratch_shapes=[pltpu.VMEM((hp + tt, tc), F32)],
        compiler_params=_cp("parallel", "parallel", "arbitrary"))(x, init, w)


def _dwconv_both(cfg, x, w, w_lead, n_ch, init_prompt, init_sample, post, out_dtype, tt_p, tc):
    tt_p = _pick(cfg.seq, tt_p, 8)
    tc = _pick(n_ch, tc)
    ms = cfg.dseq * cfg.dbatch
    yp = _dwconv(x, init_prompt, w, w_lead, n_ch=n_ch, row_off_blocks=0, batch=cfg.batch, rows=cfg.seq,
                 tstride=1, tt=tt_p, tc=tc, post=post, out_dtype=out_dtype)
    assert cfg.m_prompt % ms == 0
    ys = _dwconv(x, init_sample, w, w_lead, n_ch=n_ch, row_off_blocks=cfg.m_prompt // ms, batch=1, rows=ms,
                 tstride=cfg.dbatch, tt=ms, tc=tc, post=post, out_dtype=out_dtype)
    return yp, ys


def _halo_inits(cfg, state, taps, n_ch):
    h = taps - 1
    hp_p = -(-h // 8) * 8
    init_p = jnp.zeros((cfg.batch, hp_p, n_ch), F32)
    init_s = jnp.swapaxes(state.astype(F32), 0, 1).reshape(1, h * cfg.dbatch, n_ch)
    return init_p, init_s


def _delta_rule(qkv, z, ba, s0, s0_lead, gparams, norm_g, *, heads, seg, n_states, steps, hpb, state_out,
                prev_states=None, name="delta_rule"):
    rows = qkv.shape[0]
    r = LANES
    nseg = r // seg
    assert n_states in (1, nseg) and (n_states == 1 or steps == 1) and (n_states == nseg or nseg == 1)
    assert heads % hpb == 0
    groups = rows // (r * steps)
    z_arr, z_off = z
    ns = n_states
    n_layers, layer = state_out
    aliased = prev_states is not None

    def body(*refs):
        q_ref, k_ref, v_ref, z_ref, ba_ref, s0_ref, gp_ref, ng_ref = refs[:8]
        o_ref, sf_ref, s_ref = refs[-3:]
        hblk = pl.program_id(1)
        ci = pl.program_id(2)

        @pl.when(ci == 0)
        def _():
            for hi in range(hpb):
                s_ref[hi * ns:(hi + 1) * ns] = s0_ref[:, hi].astype(F32)

        bav = ba_ref[...]
        gp = gp_ref[...]
        lane = lax.broadcasted_iota(jnp.int32, (r, LANES), 1)
        beta_all = jax.nn.sigmoid(bav)
        xs = bav + gp[1:2, :]
        softplus = jnp.maximum(xs, 0.0) + jnp.log1p(jnp.exp(-jnp.abs(xs)))
        g_all = -jnp.exp(gp[0:1, :]) * softplus
        row = lax.broadcasted_iota(jnp.int32, (r, r), 0)
        col = lax.broadcasted_iota(jnp.int32, (r, r), 1)
        same = (row // seg) == (col // seg)
        incl = jnp.logical_and(same, row >= col)
        strict = jnp.logical_and(same, row > col)
        last = jnp.logical_and(same, col % seg == seg - 1)
        eye = jnp.where(row == col, 1.0, 0.0)
        offs = []
        s = 1
        while s < seg:
            rb = row // s
            offs.append(jnp.logical_and(rb % 2 == 1, (col // s) == rb - 1))
            s *= 2
        rseg = lax.broadcasted_iota(jnp.int32, (r, LANES), 0) // seg
        ng = ng_ref[...]

        hs = range(hpb)
        ls = [slice(hi * LANES, (hi + 1) * LANES) for hi in hs]
        q = [q_ref[:, ls[hi]] for hi in hs]
        k = [k_ref[:, ls[hi]] for hi in hs]
        v = [v_ref[:, ls[hi]] for hi in hs]
        beta = [jnp.sum(jnp.where(lane == hblk * hpb + hi, beta_all, 0.0), axis=-1, keepdims=True) for hi in hs]
        g = [jnp.sum(jnp.where(lane == hblk * hpb + hi + heads, g_all, 0.0), axis=-1, keepdims=True) for hi in hs]
        gc = [jnp.sum(jnp.where(incl, jnp.broadcast_to(g[hi], (r, r)).T, 0.0), axis=-1, keepdims=True)
              for hi in hs]
        gi = [jnp.broadcast_to(gc[hi], (r, r)) for hi in hs]
        gj = [gi[hi].T for hi in hs]
        decay = [jnp.where(incl, jnp.exp(jnp.where(incl, gi[hi] - gj[hi], 0.0)), 0.0) for hi in hs]
        kb = [k[hi] * beta[hi] for hi in hs]
        kk = [_bdot_nt(kb[hi], k[hi]) for hi in hs]
        m_mat = [jnp.where(strict, kk[hi] * decay[hi], 0.0) for hi in hs]
        t_inv = [eye - jnp.where(offs[0], m_mat[hi], 0.0) for hi in hs] if offs else [eye for _ in hs]
        for off in offs[1:]:
            left = [_bdot(t_inv[hi], jnp.where(off, m_mat[hi], 0.0)) for hi in hs]
            corr = [_bdot(left[hi], t_inv[hi]) for hi in hs]
            t_inv = [t_inv[hi] - corr[hi] for hi in hs]
        u = [_bdot(t_inv[hi], v[hi] * beta[hi]) for hi in hs]
        w = [_bdot(t_inv[hi], kb[hi] * jnp.exp(gi[hi])) for hi in hs]
        attn = [_bdot_nt(q[hi], k[hi]) * decay[hi] for hi in hs]
        qg = [q[hi] * jnp.exp(gi[hi]) for hi in hs]
        gl = [jnp.sum(jnp.where(last, gj[hi], 0.0), axis=-1, keepdims=True) for hi in hs]
        kd_t = [(k[hi] * jnp.exp(gl[hi] - gc[hi])).T for hi in hs]
        egl = [jnp.exp(gl[hi]) for hi in hs]
        if ns == 1:
            st = [s_ref[hi] for hi in hs]
            ws = [_bdot(w[hi], st[hi]) for hi in hs]
            dlt = [u[hi] - ws[hi] for hi in hs]
            qs = [_bdot(qg[hi], st[hi]) for hi in hs]
            ad = [_bdot(attn[hi], dlt[hi]) for hi in hs]
            o = [qs[hi] + ad[hi] for hi in hs]
            upd = [_bdot(kd_t[hi], dlt[hi]) for hi in hs]
            for hi in hs:
                s_ref[hi] = st[hi] * egl[hi][r - 1:r, :] + upd[hi]
        else:
            segs = range(nseg)
            wq = [[_bdot(jnp.concatenate([w[hi][si * seg:(si + 1) * seg], qg[hi][si * seg:(si + 1) * seg]],
                                         axis=0), s_ref[hi * ns + si]) for si in segs] for hi in hs]
            dlt = [u[hi] - jnp.concatenate([x[:seg] for x in wq[hi]], axis=0) for hi in hs]
            ad = [_bdot(attn[hi], dlt[hi]) for hi in hs]
            o = [jnp.concatenate([x[seg:] for x in wq[hi]], axis=0) + ad[hi] for hi in hs]
            for hi in hs:
                for si in segs:
                    upd = _bdot(kd_t[hi], jnp.where(rseg == si, dlt[hi], 0.0))
                    s_ref[hi * ns + si] = s_ref[hi * ns + si] * egl[hi][si * seg:si * seg + 1, :] + upd
        for hi in hs:
            on = o[hi] * lax.rsqrt(jnp.mean(o[hi] * o[hi], axis=-1, keepdims=True) + EPS) * ng
            o_ref[:, ls[hi]] = (on * _silu(z_ref[:, ls[hi]])).astype(BF16)

        @pl.when(ci == steps - 1)
        def _():
            for hi in range(hpb):
                sf_ref[:, hi] = s_ref[hi * ns:(hi + 1) * ns]

    hh = heads // hpb
    wl = hpb * LANES
    zb = z_off // hpb
    assert z_off % hpb == 0
    in_specs = [pl.BlockSpec((r, wl), lambda b, h, ci: (b * steps + ci, h)),
                pl.BlockSpec((r, wl), lambda b, h, ci: (b * steps + ci, hh + h)),
                pl.BlockSpec((r, wl), lambda b, h, ci: (b * steps + ci, 2 * hh + h)),
                pl.BlockSpec((r, wl), lambda b, h, ci: (b * steps + ci, zb + h)),
                pl.BlockSpec((r, LANES), lambda b, h, ci: (b * steps + ci, 0)),
                pl.BlockSpec((None,) * len(s0_lead) + (ns, hpb, LANES, LANES),
                             lambda b, h, ci: s0_lead + (b, h, 0, 0)),
                pl.BlockSpec((8, LANES), lambda b, h, ci: (0, 0)),
                pl.BlockSpec((1, LANES), lambda b, h, ci: (0, 0))]
    args = [qkv, qkv, qkv, z_arr, ba, s0, gparams, norm_g]
    aliases = {}
    if aliased:
        in_specs.append(pl.BlockSpec(memory_space=pl.ANY))
        args.append(prev_states)
        aliases = {len(args) - 1: 1}
    return pl.pallas_call(
        body, grid=(groups, heads // hpb, steps), in_specs=in_specs,
        out_specs=[pl.BlockSpec((r, wl), lambda b, h, ci: (b * steps + ci, h)),
                   pl.BlockSpec((None, ns, hpb, LANES, LANES), lambda b, h, ci: (layer, b, h, 0, 0))],
        out_shape=[jax.ShapeDtypeStruct((rows, heads * LANES), BF16),
                   jax.ShapeDtypeStruct((n_layers, groups * ns, heads, LANES, LANES), F32)],
        scratch_shapes=[pltpu.VMEM((hpb * ns, LANES, LANES), F32)],
        input_output_aliases=aliases, name=name,
        compiler_params=_cp("parallel", "parallel", "arbitrary"))(*args)


def _rope_tables(cfg):
    half = cfg.rope // 2
    inv = ROPE_THETA ** (-jnp.arange(0, cfg.rope, 2, dtype=F32) / cfg.rope)
    pos = jnp.concatenate([jnp.tile(jnp.arange(cfg.seq), cfg.batch),
                           jnp.repeat(cfg.n_pages * cfg.page + jnp.arange(cfg.dseq), cfg.dbatch)])
    ang = pos.astype(F32)[:, None] * inv[None, :]
    cos, sin = jnp.cos(ang), jnp.sin(ang)
    zpad = jnp.zeros((cfg.m, LANES - cfg.rope), F32)
    zh = jnp.zeros((cfg.m, half), F32)
    cosf = jnp.concatenate([cos, cos, zpad], axis=1)
    sin_a = jnp.concatenate([-sin, zh, zpad], axis=1)
    sin_b = jnp.concatenate([zh, sin, zpad], axis=1)
    return cosf, sin_a, sin_b


def _rope128(x, cosf, sin_a, sin_b, half):
    return (x * cosf + pltpu.roll(x, LANES - half, 1) * sin_a + pltpu.roll(x, half, 1) * sin_b)


def _mla_cq(cfg, h, w_dq, qa_g3, i):
    m = h.shape[0]
    n = cfg.q_lora
    tm = _pick(m, 1024, 8)
    tk = _pick(cfg.d, 512)

    def ep(accs, e_refs, o_refs):
        y = accs[0]
        y = y * lax.rsqrt(jnp.mean(y * y, axis=-1, keepdims=True) + EPS) * e_refs[0][...]
        o_refs[0][...] = y.astype(BF16)

    return _mm(h, [(w_dq, (i,), 0)], grid_m=m // tm, n_blocks=1, tm=tm, tn=n, tk=tk,
               outs=[((m, n), BF16, (tm, n), lambda i_, j: (i_, 0))],
               extras=[(qa_g3, (None, 1, n), lambda i_, j: (i, 0, 0))], epilogue=ep)[0]


def _mla_q(cfg, cq, w_uq_p, gq, gk, tables):
    m = cq.shape[0]
    hw = 2 * LANES
    tm = _pick(m, 1024, 8)
    qk_dim = cfg.nope + cfg.rope
    half = cfg.rope // 2
    scale = qk_dim ** -0.5

    def ep(accs, e_refs, o_refs):
        cosf, sa, sb, gq_r, gk_r = [r[...] for r in e_refs]
        acc = accs[0]
        rot = _rope128(acc[:, LANES:], cosf, sa, sb, half)
        qf = jnp.concatenate([acc[:, :LANES], rot], axis=-1)
        ssq = jnp.sum(qf * qf, axis=-1, keepdims=True) * (1.0 / qk_dim)
        o_refs[0][...] = (qf * lax.rsqrt(ssq + EPS) * gq_r * (gk_r * scale)).astype(BF16)

    tab = [(t, (tm, LANES), lambda i, j: (i, 0)) for t in tables]
    gains = [(g, (1, hw), lambda i, j: (0, 0)) for g in (gq, gk)]
    return _mm(cq, [(w_uq_p, (), 0)], grid_m=m // tm, n_blocks=cfg.heads, tm=tm, tn=hw, tk=cfg.q_lora,
               outs=[((m, cfg.heads * hw), BF16, (tm, hw), lambda i, j: (i, j))],
               extras=tab + gains, epilogue=ep)[0]


def _mla_ckv(cfg, h, w_dkv_p, kva_g3, i, tables):
    m = h.shape[0]
    kv = cfg.kv_lora
    n = kv + LANES
    tm = _pick(m, 512, 8)
    tk = _pick(cfg.d, 512)
    half = cfg.rope // 2

    def ep(accs, e_refs, o_refs):
        cosf, sa, sb, g = [r[...] for r in e_refs]
        acc = accs[0]
        c = acc[:, :kv]
        o_refs[0][...] = c * lax.rsqrt(jnp.mean(c * c, axis=-1, keepdims=True) + EPS) * g
        o_refs[1][...] = _rope128(acc[:, kv:], cosf, sa, sb, half)

    tab = [(t, (tm, LANES), lambda i_, j: (i_, 0)) for t in tables]
    return _mm(h, [(w_dkv_p, (), 0)], grid_m=m // tm, n_blocks=1, tm=tm, tn=n, tk=tk,
               outs=[((m, kv), F32, (tm, kv), lambda i_, j: (i_, 0)),
                     ((m, LANES), F32, (tm, LANES), lambda i_, j: (i_, 0))],
               extras=tab + [(kva_g3, (None, 1, kv), lambda i_, j: (i, 0, 0))], epilogue=ep)


def _mla_prompt_kv(cfg, ckv, krope_p, w_ukv, i):
    hw = 2 * LANES
    mp = cfg.m_prompt
    tm = _pick(mp, 1024, 8)
    qk_dim = cfg.nope + cfg.rope

    def ep(accs, e_refs, o_refs):
        acc = accs[0]
        kr = e_refs[0][...]
        kn = acc[:, :LANES]
        ssq = (jnp.sum(kn * kn, axis=-1, keepdims=True) + jnp.sum(kr * kr, axis=-1, keepdims=True))
        s = lax.rsqrt(ssq * (1.0 / qk_dim) + EPS)
        o_refs[0][...] = jnp.concatenate([kn * s, kr * s], axis=-1).astype(BF16)
        o_refs[1][...] = acc[:, LANES:].astype(BF16)

    return _mm(ckv, [(w_ukv, (i,), 0)], grid_m=mp // tm, n_blocks=cfg.heads, tm=tm, tn=hw, tk=cfg.kv_lora,
               outs=[((mp, cfg.heads * hw), BF16, (tm, hw), lambda i_, j: (i_, j)),
                     ((mp, cfg.heads * LANES), BF16, (tm, LANES), lambda i_, j: (i_, j))],
               extras=[(krope_p, (tm, LANES), lambda i_, j: (i_, 0))], epilogue=ep)


def _flash_prompt(cfg, q, k, v):
    tq = _pick(cfg.seq, 512, 8)
    nq = cfg.seq // tq
    hw = 2 * LANES
    hpb = _pick(cfg.heads, 4, 1)
    hs = range(hpb)

    def body(q_ref, k_ref, v_ref, o_ref, m_ref, l_ref, acc_ref):
        qi = pl.program_id(2)
        kj = pl.program_id(3)

        @pl.when(kj == 0)
        def _():
            m_ref[...] = jnp.full(m_ref.shape, NEG, F32)
            l_ref[...] = jnp.zeros(l_ref.shape, F32)
            acc_ref[...] = jnp.zeros(acc_ref.shape, F32)

        @pl.when(kj <= qi)
        def _():
            qpos = qi * tq + lax.broadcasted_iota(jnp.int32, (tq, tq), 0)
            kpos = kj * tq + lax.broadcasted_iota(jnp.int32, (tq, tq), 1)
            visible = kpos <= qpos
            s = [lax.dot_general(q_ref[:, h * hw:(h + 1) * hw], k_ref[:, h * hw:(h + 1) * hw],
                                 (((1,), (1,)), ((), ())), preferred_element_type=F32) for h in hs]
            s = [jnp.where(visible, s[h], NEG) for h in hs]
            m_prev = [m_ref[h] for h in hs]
            m_new = [jnp.maximum(m_prev[h], jnp.max(s[h], axis=-1, keepdims=True)) for h in hs]
            alpha = [jnp.exp(m_prev[h] - m_new[h]) for h in hs]
            p = [jnp.exp(s[h] - m_new[h]) for h in hs]
            pv = [jnp.dot(p[h].astype(BF16), v_ref[:, h * LANES:(h + 1) * LANES], preferred_element_type=F32)
                  for h in hs]
            for h in hs:
                l_ref[h] = alpha[h] * l_ref[h] + jnp.sum(p[h], axis=-1, keepdims=True)
                acc_ref[h] = alpha[h] * acc_ref[h] + pv[h]
                m_ref[h] = m_new[h]

        @pl.when(kj == nq - 1)
        def _():
            for h in hs:
                o_ref[:, h * LANES:(h + 1) * LANES] = (acc_ref[h] / l_ref[h]).astype(BF16)

    return pl.pallas_call(
        body, grid=(cfg.batch, cfg.heads // hpb, nq, nq),
        in_specs=[pl.BlockSpec((tq, hpb * hw), lambda b, h, qi, kj: (b * nq + qi, h)),
                  pl.BlockSpec((tq, hpb * hw), lambda b, h, qi, kj: (b * nq + jnp.minimum(kj, qi), h)),
                  pl.BlockSpec((tq, hpb * LANES), lambda b, h, qi, kj: (b * nq + jnp.minimum(kj, qi), h))],
        out_specs=pl.BlockSpec((tq, hpb * LANES), lambda b, h, qi, kj: (b * nq + qi, h)),
        out_shape=jax.ShapeDtypeStruct((cfg.m_prompt, cfg.heads * LANES), BF16),
        scratch_shapes=[pltpu.VMEM((hpb, tq, 1), F32), pltpu.VMEM((hpb, tq, 1), F32),
                        pltpu.VMEM((hpb, tq, LANES), F32)],
        name="flash_prompt", compiler_params=_cp("parallel", "parallel", "parallel", "arbitrary"))(q, k, v)


def _head_mm(a, w, *, a_block_of_head, kdim, n, out_dtype):
    rows = a.shape[0]
    heads = w.shape[0]

    def body(a_ref, w_ref, o_ref):
        o_ref[...] = jnp.dot(a_ref[...].astype(BF16), w_ref[...], preferred_element_type=F32).astype(out_dtype)

    return pl.pallas_call(
        body, grid=(heads,),
        in_specs=[pl.BlockSpec((rows, kdim), lambda h: (0, a_block_of_head(h))),
                  pl.BlockSpec((None, kdim, n), lambda h: (h, 0, 0))],
        out_specs=pl.BlockSpec((rows, n), lambda h: (0, h)),
        out_shape=jax.ShapeDtypeStruct((rows, heads * n), out_dtype),
        compiler_params=_cp("parallel"))(a, w)


def _decode_attention(cfg, page_table, cache_ckv, cache_krope_t, layer, qabs, qr, c_new, kr_new_t, w_uk_t):
    db, hq, kv = qabs.shape
    tq = cfg.dseq
    heads = cfg.heads
    npg = cfg.n_pages
    ppk = _pick(npg, 8, 2)
    assert npg % ppk == 0 and ppk % 2 == 0 and hq == heads * tq and tq == 8
    nsteps = npg // ppk
    page = cfg.page
    sub = 2 * page
    qk_dim = cfg.nope + cfg.rope
    hb_rows = min(heads, 8) * cfg.nope

    def body(pt_ref, *refs):
        c_refs = refs[:ppk]
        r_refs = refs[ppk:2 * ppk]
        qa_ref, qr_ref, cn_ref, rn_ref, wt_ref, o_ref, m_ref, l_ref, acc_ref, s_ref = refs[2 * ppk:]
        p = pl.program_id(1)

        @pl.when(p == 0)
        def _():
            m_ref[...] = jnp.full(m_ref.shape, NEG, F32)
            l_ref[...] = jnp.zeros(l_ref.shape, F32)
            acc_ref[...] = jnp.zeros(acc_ref.shape, F32)

        def process(cb, kr_t, new_rows):
            nkeys = cb.shape[0]
            ssq_r = jnp.sum(kr_t * kr_t, axis=0, keepdims=True)
            s_all = _bdot_nt(qa_ref[...], cb) + _bdot(qr_ref[...], kr_t)
            for lo in range(0, nkeys, sub):
                hi_ = min(lo + sub, nkeys)
                for hb in range(heads * cfg.nope // hb_rows):
                    kt = _bdot_nt(wt_ref[hb * hb_rows:(hb + 1) * hb_rows, :], cb[lo:hi_])
                    for hh in range(hb_rows // cfg.nope):
                        hd = hb * (hb_rows // cfg.nope) + hh
                        sl = kt[hh * cfg.nope:(hh + 1) * cfg.nope, :]
                        ssq = jnp.sum(sl * sl, axis=0, keepdims=True) + ssq_r[:, lo:hi_]
                        ksc = lax.rsqrt(ssq * (1.0 / qk_dim) + EPS)
                        s_ref[hd * tq:(hd + 1) * tq, lo:hi_] = s_all[hd * tq:(hd + 1) * tq, lo:hi_] * ksc
            s = s_ref[:, 0:nkeys]
            if new_rows:
                qrow = lax.broadcasted_iota(jnp.int32, (hq, nkeys), 0) % tq
                kcol = lax.broadcasted_iota(jnp.int32, (hq, nkeys), 1)
                s = jnp.where(kcol <= qrow, s, NEG)
            m_prev = m_ref[...]
            m_new = jnp.maximum(m_prev, jnp.max(s, axis=-1, keepdims=True))
            alpha = jnp.exp(m_prev - m_new)
            pr = jnp.exp(s - m_new)
            l_ref[...] = alpha * l_ref[...] + jnp.sum(pr, axis=-1, keepdims=True)
            acc_ref[...] = alpha * acc_ref[...] + jnp.dot(pr.astype(BF16), cb, preferred_element_type=F32)
            m_ref[...] = m_new

        @pl.when(p < nsteps)
        def _():
            process(jnp.concatenate([c[...].astype(BF16) for c in c_refs], axis=0),
                    jnp.concatenate([r_[...] for r_ in r_refs], axis=1), False)

        @pl.when(p == nsteps)
        def _():
            process(cn_ref[...].astype(BF16), rn_ref[...], True)
            o_ref[...] = acc_ref[...] / l_ref[...]

    def pg(which):
        def imap(b, p, pt):
            pp = jnp.minimum(p, nsteps - 1)
            return (layer, pt[b * npg + ppk * pp + which], 0, 0)
        return imap

    grid_spec = pltpu.PrefetchScalarGridSpec(
        num_scalar_prefetch=1, grid=(db, nsteps + 1),
        in_specs=[pl.BlockSpec((None, None, page, kv), pg(w_)) for w_ in range(ppk)]
        + [pl.BlockSpec((None, None, cfg.rope, page), pg(w_)) for w_ in range(ppk)]
        + [pl.BlockSpec((None, hq, kv), lambda b, p, pt: (b, 0, 0)),
                  pl.BlockSpec((None, hq, cfg.rope), lambda b, p, pt: (b, 0, 0)),
                  pl.BlockSpec((None, LANES, kv), lambda b, p, pt: (b, 0, 0)),
                  pl.BlockSpec((None, cfg.rope, LANES), lambda b, p, pt: (b, 0, 0)),
                  pl.BlockSpec((heads * cfg.nope, kv), lambda b, p, pt: (0, 0))],
        out_specs=pl.BlockSpec((None, hq, kv), lambda b, p, pt: (b, 0, 0)),
        scratch_shapes=[pltpu.VMEM((hq, 1), F32), pltpu.VMEM((hq, 1), F32), pltpu.VMEM((hq, kv), F32),
                        pltpu.VMEM((hq, ppk * page), F32)])
    return pl.pallas_call(
        body, grid_spec=grid_spec, out_shape=jax.ShapeDtypeStruct((db, hq, kv), F32),
        name="decode_attention", compiler_params=_cp("parallel", "arbitrary"))(
            page_table.reshape(-1), *([cache_ckv] * ppk), *([cache_krope_t] * ppk),
            qabs, qr, c_new, kr_new_t, w_uk_t)


def _to_batch_major(cfg, rows):
    return jnp.swapaxes(rows.reshape(cfg.dseq, cfg.dbatch, rows.shape[-1]), 0, 1)


def _to_time_major(cfg, arr):
    return jnp.swapaxes(arr, 0, 1).reshape(cfg.dseq * cfg.dbatch, arr.shape[-1])


def _last_rows_prompt(cfg, rows, n_last, n_ch):
    return jnp.stack([lax.slice(rows, ((b + 1) * cfg.seq - n_last, 0), ((b + 1) * cfg.seq, n_ch))
                      for b in range(cfg.batch)])


def _last_rows_sample(cfg, rows, state, n_last, n_ch):
    mp, g, t = cfg.m_prompt, cfg.dbatch, cfg.dseq
    keep = max(n_last - t, 0)
    new = lax.slice(rows, (mp + max(t - n_last, 0) * g, 0), (mp + t * g, n_ch))
    new = jnp.swapaxes(new.reshape(-1, g, n_ch), 0, 1)
    if keep == 0:
        return new
    return jnp.concatenate([state[:, n_last - keep:].astype(F32), new], axis=1)


def _ffn(cfg, x, mod, norm_g3, l, sub, which, wg, wu, wd):
    h = _adanorm(cfg, x, norm_g3, 3 * l + sub, mod, sub)
    m, d = x.shape
    dff = wg.shape[-1]
    tm = _pick(m, 1024, 8)
    tn = _pick(dff, 256)
    tk = _pick(d, 4096)
    act = _mm(h, [(wg, (l, which), 0), (wu, (l, which), 0)], grid_m=m // tm, n_blocks=dff // tn,
              tm=tm, tn=tn, tk=tk, outs=[((m, dff), BF16, (tm, tn), lambda i, j: (i, j))],
              epilogue=_ep_swiglu, name="ffn_up")[0]
    return _resid_mm(cfg, act, wd, (l, which), x, mod, 3 * sub + 2, 0.5, name="ffn_down")


def _deltanet_layer(cfg, h, i, state_delta, state_delta_conv, dn_w_in, dn_conv_w, dn_a_log, dn_dt_bias,
                    dn_norm_g, prompt_chunk, prev_p, prev_s):
    heads, dk = cfg.heads, cfg.dk
    d_a = heads * dk
    n_qkvz = 4 * d_a
    mp = cfg.m_prompt
    w_in_t = jnp.swapaxes(dn_w_in, 1, 2)
    proj = _plain_mm(cfg, h, w_in_t, (i,), n=n_qkvz, w_nt=True, name="dn_in_proj")
    w_ba_t = jnp.pad(w_in_t[i, n_qkvz:, :], ((0, LANES - 2 * heads), (0, 0)))
    ba = _plain_mm(cfg, h, w_ba_t, (), n=LANES, tn=LANES, w_nt=True, name="dn_ba_proj")

    nq = d_a
    tc = _pick(3 * d_a, 512)

    def post(y, cb):
        y = _silu(y)
        is_qk = cb < (2 * nq) // tc
        is_q = cb < nq // tc
        outs = []
        for s in range(tc // dk):
            ys = y[:, s * dk:(s + 1) * dk]
            ssq = jnp.sum(ys * ys, axis=-1, keepdims=True)
            sc = jnp.where(is_qk, lax.rsqrt(ssq + EPS), 1.0) * jnp.where(is_q, dk ** -0.5, 1.0)
            outs.append(ys * sc)
        return jnp.concatenate(outs, axis=-1)

    taps = dn_conv_w.shape[1]
    init_p, init_s = _halo_inits(cfg, state_delta_conv[i], taps, 3 * d_a)
    qkv_p, qkv_s = _dwconv_both(cfg, proj, dn_conv_w, (i,), 3 * d_a, init_p, init_s, post, F32, 256, tc)

    gparams = jnp.zeros((8, LANES), F32)
    gparams = gparams.at[0, heads:2 * heads].set(dn_a_log[i].astype(F32))
    gparams = gparams.at[1, heads:2 * heads].set(dn_dt_bias[i].astype(F32))
    ng = dn_norm_g[i].astype(F32).reshape(1, dk)

    s0_p = jnp.zeros((cfg.batch, heads, dk, dk), F32)
    n_layers = state_delta.shape[0]
    o_p, s_p = _delta_rule(qkv_p, (proj, 3 * heads), ba, s0_p, (), gparams, ng,
                           heads=heads, seg=prompt_chunk, n_states=1, steps=cfg.seq // LANES,
                           hpb=_pick(heads, 8, 1), state_out=(n_layers, i), prev_states=prev_p,
                           name="delta_rule_prompt")
    ms = cfg.dseq * cfg.dbatch
    o_s, s_s = _delta_rule(_to_batch_major(cfg, qkv_s).reshape(ms, 3 * d_a),
                           (_to_batch_major(cfg, proj[mp:, 3 * d_a:]).reshape(ms, d_a), 0),
                           _to_batch_major(cfg, ba[mp:]).reshape(ms, LANES), state_delta, (i,), gparams, ng,
                           heads=heads, seg=cfg.dseq, n_states=LANES // cfg.dseq, steps=1,
                           hpb=_pick(heads, 2, 1), state_out=(n_layers, i), prev_states=prev_s,
                           name="delta_rule_sample")
    o_all = jnp.concatenate([o_p, _to_time_major(cfg, o_s.reshape(cfg.dbatch, cfg.dseq, d_a))], axis=0)
    new_conv_p = _last_rows_prompt(cfg, proj, taps - 1, 3 * d_a)
    new_conv_s = _last_rows_sample(cfg, proj, state_delta_conv[i], taps - 1, 3 * d_a)
    return o_all, s_p, s_s, new_conv_p, new_conv_s


def _conformer_layer(cfg, h, i, state_conv, cv_w_pw1, cv_b_pw1, cv_w_dw, cv_b_dw, cv_ln_g, cv_ln_b):
    m, d = h.shape
    dc = cv_w_dw.shape[-1]
    tm = _pick(m, 1024, 8)
    tn = _pick(dc, 256)
    tk = _pick(d, 4096)
    b3 = cv_b_pw1.reshape(cv_b_pw1.shape[0], 1, 2 * dc)
    u = _mm(h, [(cv_w_pw1, (i,), 0), (cv_w_pw1, (i,), dc // tn)], grid_m=m // tm, n_blocks=dc // tn,
            tm=tm, tn=tn, tk=tk, outs=[((m, dc), F32, (tm, tn), lambda i_, j: (i_, j))],
            extras=[(b3, (None, 1, tn), lambda i_, j: (i, 0, j)),
                    (b3, (None, 1, tn), lambda i_, j: (i, 0, dc // tn + j))],
            epilogue=_ep_glu, name="conformer_glu")[0]
    taps = cv_w_dw.shape[1]
    init_p, init_s = _halo_inits(cfg, state_conv[i], taps, dc)
    yp, ys = _dwconv_both(cfg, u, cv_w_dw, (i,), dc, init_p, init_s, lambda y, cb: y, F32, 128, 256)
    y = jnp.concatenate([yp, ys], axis=0) + cv_b_dw[i][None, :]
    hn = _layernorm_silu(y, cv_ln_g.reshape(-1, 1, dc), cv_ln_b.reshape(-1, 1, dc), i)
    new_p = _last_rows_prompt(cfg, u, taps - 1, dc)
    new_s = _last_rows_sample(cfg, u, state_conv[i], taps - 1, dc)
    return hn, new_p, new_s


def _mla_layer(cfg, h, i, cache_ckv, cache_krope, page_table, tables, m_w_dq, m_qa_g, m_w_uq, m_w_dkv,
               m_kva_g, m_w_ukv, m_q_norm_g, m_k_norm_g):
    heads, nope, rope, kv = cfg.heads, cfg.nope, cfg.rope, cfg.kv_lora
    qk_dim = nope + rope
    hw = 2 * LANES
    mp = cfg.m_prompt
    half = rope // 2

    def tied(g):
        g = g.astype(F32)
        return jnp.concatenate([g[:nope], g[nope:], g[nope:], jnp.zeros((hw - qk_dim,), F32)]).reshape(1, hw)

    w_uq_p = jnp.pad(m_w_uq[i].reshape(cfg.q_lora, heads, qk_dim),
                     ((0, 0), (0, 0), (0, hw - qk_dim))).reshape(cfg.q_lora, heads * hw)
    w_dkv_p = jnp.pad(m_w_dkv[i], ((0, 0), (0, LANES - rope)))
    cq = _mla_cq(cfg, h, m_w_dq, m_qa_g.reshape(-1, 1, cfg.q_lora), i)
    q = _mla_q(cfg, cq, w_uq_p, tied(m_q_norm_g[i]), tied(m_k_norm_g[i]), tables)
    ckv, krope_p = _mla_ckv(cfg, h, w_dkv_p, m_kva_g.reshape(-1, 1, kv), i, tables)

    k_p, v_p = _mla_prompt_kv(cfg, ckv, krope_p, m_w_ukv, i)
    att_p = _flash_prompt(cfg, q, k_p, v_p)

    w3 = m_w_ukv[i].reshape(kv, heads, nope + LANES)
    w_uk_h = jnp.transpose(w3[:, :, :nope], (1, 2, 0)).astype(BF16)
    w_uv_h = jnp.transpose(w3[:, :, nope:], (1, 0, 2)).astype(BF16)
    q_s = q[mp:]
    ms = q_s.shape[0]
    qabs = _head_mm(q_s, w_uk_h, a_block_of_head=lambda hd: 2 * hd, kdim=nope, n=kv, out_dtype=BF16)
    qabs = jnp.transpose(qabs.reshape(cfg.dseq, cfg.dbatch, heads, kv), (1, 2, 0, 3)).reshape(
        cfg.dbatch, heads * cfg.dseq, kv)
    qr = q_s.reshape(cfg.dseq, cfg.dbatch, heads, hw)[..., nope:qk_dim]
    qr = jnp.transpose(qr, (1, 2, 0, 3)).reshape(cfg.dbatch, heads * cfg.dseq, rope)
    ckv_s = _to_batch_major(cfg, ckv[mp:])
    kr_s = _to_batch_major(cfg, krope_p[mp:, :rope])
    c_new = jnp.pad(ckv_s, ((0, 0), (0, LANES - cfg.dseq), (0, 0)))
    kr_new_t = jnp.pad(jnp.swapaxes(kr_s, 1, 2), ((0, 0), (0, 0), (0, LANES - cfg.dseq)))
    o_lat = _decode_attention(cfg, page_table, cache_ckv, jnp.swapaxes(cache_krope, 2, 3), i, qabs, qr, c_new,
                              kr_new_t,
                              w_uk_h.reshape(heads * nope, kv))
    o_lat = jnp.transpose(o_lat.reshape(cfg.dbatch, heads, cfg.dseq, kv), (2, 0, 1, 3)).reshape(ms, heads * kv)
    att_s = _head_mm(o_lat, w_uv_h, a_block_of_head=lambda hd: hd, kdim=kv, n=LANES, out_dtype=BF16)
    att = jnp.concatenate([att_p, att_s], axis=0)
    ckv_p_out = ckv[:mp].reshape(cfg.batch, cfg.seq, kv)
    kr_p_out = krope_p[:mp, :rope].reshape(cfg.batch, cfg.seq, rope)
    return att, ckv_p_out, kr_p_out, ckv_s, kr_s


def kernel(x_prompt, x_sample, cache_ckv, cache_krope, state_delta, state_delta_conv, state_conv, page_table,
           c_prompt, c_sample, ada_w, ada_b, norm_g, ffn_wg, ffn_wu, ffn_wd, dn_w_in, dn_conv_w, dn_a_log,
           dn_dt_bias, dn_norm_g, dn_w_out, cv_w_pw1, cv_b_pw1, cv_w_dw, cv_b_dw, cv_ln_g, cv_ln_b, cv_w_pw2,
           cv_b_pw2, m_w_dq, m_qa_g, m_w_uq, m_w_dkv, m_kva_g, m_w_ukv, m_q_norm_g, m_k_norm_g, m_w_o):
    batch, seq, d = x_prompt.shape
    dbatch, dseq, _ = x_sample.shape
    depth = ada_w.shape[0]
    rope = cache_krope.shape[-1]
    kv_lora = cache_ckv.shape[-1]
    heads = dn_a_log.shape[-1]
    nope = m_q_norm_g.shape[-1] - rope // 2
    cfg = Cfg(batch=batch, seq=seq, dbatch=dbatch, dseq=dseq, d=d, m_prompt=batch * seq,
              m=batch * seq + dbatch * dseq, n_pages=page_table.shape[1], page=cache_ckv.shape[2],
              heads=heads, dk=state_delta.shape[-2], nope=nope, rope=rope, kv_lora=kv_lora,
              q_lora=m_w_dq.shape[-1])
    assert dbatch % 8 == 0 and seq % dbatch == 0 and cfg.dk == LANES and nope == LANES

    x = jnp.concatenate([x_prompt.reshape(batch * seq, d).astype(F32),
                         jnp.swapaxes(x_sample.astype(F32), 0, 1).reshape(dseq * dbatch, d)], axis=0)
    c_all = jnp.concatenate([jnp.repeat(c_prompt, dbatch, axis=0), c_sample], axis=0).astype(F32)
    norm_g3 = norm_g.reshape(depth * 3, 1, d).astype(F32)
    ada_b3 = ada_b.reshape(depth, 1, N_ADA * d)
    tables = _rope_tables(cfg)
    mrows = c_all.shape[0]
    tn_ada = _pick(N_ADA * d, 512)
    tk_ada = _pick(d, 4096)

    def silu_body(c_ref, o_ref):
        o_ref[...] = _silu(c_ref[...]).astype(BF16)

    c_act = pl.pallas_call(silu_body, out_shape=jax.ShapeDtypeStruct(c_all.shape, BF16), name="ada_silu")(c_all)

    def ep_bias(accs, e_refs, o_refs):
        o_refs[0][...] = accs[0] + e_refs[0][...]

    outs = {k: [] for k in ("ckv_p", "kr_p", "ckv_s", "kr_s", "dnc_p", "dnc_s", "cv_p", "cv_s")}
    dn_p = dn_s = None
    for l in range(depth):
        mod = _mm(c_act, [(ada_w, (l,), 0)], grid_m=1, n_blocks=N_ADA * d // tn_ada, tm=mrows, tn=tn_ada,
                  tk=tk_ada, outs=[((mrows, N_ADA * d), F32, (mrows, tn_ada), lambda i, j: (0, j))],
                  extras=[(ada_b3, (None, 1, tn_ada), lambda i, j, l=l: (l, 0, j))], epilogue=ep_bias,
                  name="ada")[0]
        x = _ffn(cfg, x, mod, norm_g3, l, 0, 0, ffn_wg, ffn_wu, ffn_wd)
        h = _adanorm(cfg, x, norm_g3, 3 * l + 1, mod, 1)
        kind, i = l % 3, l // 3
        if kind == 0:
            o, dn_p, dn_s, nc_p, nc_s = _deltanet_layer(cfg, h, i, state_delta, state_delta_conv, dn_w_in,
                                                        dn_conv_w, dn_a_log, dn_dt_bias, dn_norm_g,
                                                        min(LANES, seq), dn_p, dn_s)
            x = _resid_mm(cfg, o, dn_w_out, (i,), x, mod, 5, 1.0)
            outs["dnc_p"].append(nc_p)
            outs["dnc_s"].append(nc_s)
        elif kind == 1:
            hn, nb_p, nb_s = _conformer_layer(cfg, h, i, state_conv, cv_w_pw1, cv_b_pw1, cv_w_dw, cv_b_dw,
                                              cv_ln_g, cv_ln_b)
            x = _resid_mm(cfg, hn, cv_w_pw2, (i,), x, mod, 5, 1.0,
                          bias=cv_b_pw2.reshape(-1, 1, d), bias_lead=(i,))
            outs["cv_p"].append(nb_p)
            outs["cv_s"].append(nb_s)
        else:
            att, ckv_p, kr_p, ckv_s, kr_s = _mla_layer(cfg, h, i, cache_ckv, cache_krope, page_table, tables,
                                                       m_w_dq, m_qa_g, m_w_uq, m_w_dkv, m_kva_g, m_w_ukv,
                                                       m_q_norm_g, m_k_norm_g)
            x = _resid_mm(cfg, att, m_w_o, (i,), x, mod, 5, 1.0)
            outs["ckv_p"].append(ckv_p)
            outs["kr_p"].append(kr_p)
            outs["ckv_s"].append(ckv_s)
            outs["kr_s"].append(kr_s)
        x = _ffn(cfg, x, mod, norm_g3, l, 2, 1, ffn_wg, ffn_wu, ffn_wd)

    mp = cfg.m_prompt
    y_prompt = x[:mp].reshape(batch, seq, d)
    y_sample = jnp.swapaxes(x[mp:].reshape(dseq, dbatch, d), 0, 1)
    st = jnp.stack
    return (y_prompt, y_sample, st(outs["ckv_p"]), st(outs["kr_p"]), st(outs["ckv_s"]), st(outs["kr_s"]),
            dn_p, dn_s, st(outs["dnc_p"]), st(outs["dnc_s"]),
            st(outs["cv_p"]), st(outs["cv_s"]))
```

```python
import collections
import functools

import jax
import jax.numpy as jnp
from jax import lax
from jax.experimental import pallas as pl
from jax.experimental.pallas import tpu as pltpu

F32 = jnp.float32
BF16 = jnp.bfloat16
EPS = 1e-6
N_ADA = 9
ROPE_THETA = 10000.0
LANES = 128
VMEM_LIMIT = 52 * 1024 * 1024
NEG = -1e30

Cfg = collections.namedtuple(
    "Cfg", "batch seq dbatch dseq d m_prompt m n_pages page heads dk nope rope kv_lora q_lora")


def _cp(*sem):
    return pltpu.CompilerParams(dimension_semantics=sem, vmem_limit_bytes=VMEM_LIMIT)


def _pick(dim, pref, mult=LANES):
    if dim <= pref:
        return dim
    t = (pref // mult) * mult
    while t >= mult:
        if dim % t == 0:
            return t
        t -= mult
    return dim


def _pick_tm(cfg, pref):
    g = cfg.dbatch
    t = (pref // g) * g
    while t > g:
        if cfg.seq % t == 0 and (cfg.dseq * g) % t == 0:
            return t
        t -= g
    return g


def _rowblk(cfg, tm):
    return lambda i: jnp.minimum((i * tm) // cfg.seq, cfg.batch)


def _bdot(a, b):
    return jnp.dot(a.astype(BF16), b.astype(BF16), preferred_element_type=F32)


def _bdot_nt(a, b):
    return lax.dot_general(a.astype(BF16), b.astype(BF16), (((1,), (1,)), ((), ())),
                           preferred_element_type=F32)


def _silu(x):
    return x * jax.nn.sigmoid(x)


def _mm(a, ws, *, grid_m, n_blocks, tm, tn, tk, outs, epilogue, extras=(), row_off=0, w_nt=False, name="mm"):
    nk = a.shape[1] // tk
    nw, ne, no = len(ws), len(extras), len(outs)
    in_specs = [pl.BlockSpec((tm, tk), lambda i, j, k: (i + row_off, k))]
    args = [a]
    for w, lead, off in ws:
        if w_nt:
            in_specs.append(pl.BlockSpec((None,) * len(lead) + (tn, tk),
                                         lambda i, j, k, lead=lead, off=off: lead + (j + off, k)))
        else:
            in_specs.append(pl.BlockSpec((None,) * len(lead) + (tk, tn),
                                         lambda i, j, k, lead=lead, off=off: lead + (k, j + off)))
        args.append(w)

    def wdot(av, w):
        dims = (((1,), (1,)), ((), ())) if w_nt else (((1,), (0,)), ((), ()))
        return lax.dot_general(av[...].astype(BF16), w[...].astype(BF16), dims, preferred_element_type=F32)
    for arr, blk, imap in extras:
        in_specs.append(pl.BlockSpec(blk, lambda i, j, k, imap=imap: imap(i, j)))
        args.append(arr)
    out_specs = [pl.BlockSpec(blk, lambda i, j, k, imap=imap: imap(i, j)) for _, _, blk, imap in outs]
    out_shape = [jax.ShapeDtypeStruct(s, d) for s, d, _, _ in outs]

    def body(*refs):
        a_ref = refs[0]
        w_refs = refs[1:1 + nw]
        e_refs = refs[1 + nw:1 + nw + ne]
        o_refs = refs[1 + nw + ne:1 + nw + ne + no]
        acc_refs = refs[1 + nw + ne + no:]
        av = a_ref
        if nk == 1:
            epilogue([wdot(av, w) for w in w_refs], e_refs, o_refs)
            return
        k = pl.program_id(2)

        @pl.when(k == 0)
        def _():
            for acc, w in zip(acc_refs, w_refs):
                acc[...] = wdot(av, w)

        if nk > 2:
            @pl.when(jnp.logical_and(k > 0, k < nk - 1))
            def _():
                for acc, w in zip(acc_refs, w_refs):
                    acc[...] += wdot(av, w)

        @pl.when(k == nk - 1)
        def _():
            epilogue([acc[...] + wdot(av, w) for acc, w in zip(acc_refs, w_refs)], e_refs, o_refs)

    scratch = [pltpu.VMEM((tm, tn), F32) for _ in range(nw)] if nk > 1 else []
    res = pl.pallas_call(
        body, grid=(grid_m, n_blocks, nk), in_specs=in_specs, out_specs=out_specs,
        out_shape=out_shape, scratch_shapes=scratch, name=name,
        compiler_params=_cp("parallel", "parallel", "arbitrary"))(*args)
    return res


def _ep_store(accs, e_refs, o_refs):
    o_refs[0][...] = accs[0].astype(o_refs[0].dtype)


def _ep_swiglu(accs, e_refs, o_refs):
    o_refs[0][...] = (_silu(accs[0]) * accs[1]).astype(o_refs[0].dtype)


def _ep_glu(accs, e_refs, o_refs):
    a = accs[0] + e_refs[0][...]
    b = accs[1] + e_refs[1][...]
    o_refs[0][...] = (a * jax.nn.sigmoid(b)).astype(o_refs[0].dtype)


def _ep_resid(coef, has_bias, accs, e_refs, o_refs):
    y = accs[0]
    if has_bias:
        y = y + e_refs[2][...]
    g = e_refs[1][...]
    tm, tn = y.shape
    gy = (y.reshape(tm // g.shape[0], g.shape[0], tn) * g[None]).reshape(tm, tn)
    o_refs[0][...] = e_refs[0][...] + coef * gy


def _plain_mm(cfg, a, w, lead, *, n, col_off=0, out_dtype=F32, tm=1024, tn=512, tk=4096, w_nt=False,
              name="mm"):
    m = a.shape[0]
    tm = _pick(m, tm, 8)
    tn = _pick(n, tn)
    tk = _pick(a.shape[1], tk)
    return _mm(a, [(w, lead, col_off // tn)], grid_m=m // tm, n_blocks=n // tn, tm=tm, tn=tn, tk=tk,
               outs=[((m, n), out_dtype, (tm, tn), lambda i, j: (i, j))], epilogue=_ep_store, w_nt=w_nt,
               name=name)[0]


def _resid_mm(cfg, a, w, lead, x, mod, gate_idx, coef, bias=None, bias_lead=(), tm=1024, tn=512, tk=4096,
              name="resid_mm"):
    m, d = x.shape
    tm = _pick_tm(cfg, tm)
    tk = _pick(a.shape[1], tk)
    if a.shape[1] > tk and bias is None:
        return _resid_mm_ksplit(cfg, a, w, lead, x, mod, gate_idx, coef, tm, _pick(d, 256), tk, name)
    tn = _pick(d, tn)
    rb = _rowblk(cfg, tm)
    goff = gate_idx * (d // tn)
    extras = [(x, (tm, tn), lambda i, j: (i, j)),
              (mod, (cfg.dbatch, tn), lambda i, j: (rb(i), goff + j))]
    if bias is not None:
        extras.append((bias, (None,) * len(bias_lead) + (1, tn), lambda i, j: bias_lead + (0, j)))
    return _mm(a, [(w, lead, 0)], grid_m=m // tm, n_blocks=d // tn, tm=tm, tn=tn, tk=tk,
               outs=[((m, d), F32, (tm, tn), lambda i, j: (i, j))], extras=extras,
               epilogue=functools.partial(_ep_resid, coef, bias is not None), name=name)[0]


def _resid_mm_ksplit(cfg, a, w, lead, x, mod, gate_idx, coef, tm, tn, tk, name):
    m, d = x.shape
    nk = a.shape[1] // tk
    nj = d // tn
    rb = _rowblk(cfg, tm)
    goff = gate_idx * nj
    g = cfg.dbatch

    def col(j, k):
        return jnp.where(k == nk - 1, j, 0)

    def dot(a_ref, w_ref):
        return jnp.dot(a_ref[...].astype(BF16), w_ref[...].astype(BF16), preferred_element_type=F32)

    def body(a_ref, w_ref, x_ref, g_ref, o_ref, acc_ref):
        k = pl.program_id(1)
        j = pl.program_id(2)

        @pl.when(k == 0)
        def _():
            acc_ref[j] = dot(a_ref, w_ref)

        if nk > 2:
            @pl.when(jnp.logical_and(k > 0, k < nk - 1))
            def _():
                acc_ref[j] += dot(a_ref, w_ref)

        @pl.when(k == nk - 1)
        def _():
            y = acc_ref[j] + dot(a_ref, w_ref)
            gy = (y.reshape(tm // g, g, tn) * g_ref[...][None]).reshape(tm, tn)
            o_ref[...] = x_ref[...] + coef * gy

    return pl.pallas_call(
        body, grid=(m // tm, nk, nj),
        in_specs=[pl.BlockSpec((tm, tk), lambda i, k, j: (i, k)),
                  pl.BlockSpec((None,) * len(lead) + (tk, tn), lambda i, k, j: lead + (k, j)),
                  pl.BlockSpec((tm, tn), lambda i, k, j: (i, col(j, k))),
                  pl.BlockSpec((g, tn), lambda i, k, j: (rb(i), goff + col(j, k)))],
        out_specs=pl.BlockSpec((tm, tn), lambda i, k, j: (i, col(j, k))),
        out_shape=jax.ShapeDtypeStruct((m, d), F32),
        scratch_shapes=[pltpu.VMEM((nj, tm, tn), F32)], name=name,
        compiler_params=_cp("parallel", "arbitrary", "arbitrary"))(a, w, x, mod)


def _adanorm(cfg, x, norm_g3, gidx, mod, sub):
    m, d = x.shape
    tm = _pick_tm(cfg, 256)
    g = cfg.dbatch
    rb = _rowblk(cfg, tm)

    def body(x_ref, g_ref, sh_ref, sc_ref, o_ref):
        chunks = [slice(c * LANES, (c + 1) * LANES) for c in range(d // LANES)]
        acc = jnp.zeros((tm, LANES), F32)
        for sl in chunks:
            xc = x_ref[:, sl]
            acc = acc + xc * xc
        inv = lax.rsqrt(jnp.sum(acc, axis=-1, keepdims=True) * (1.0 / d) + EPS)
        for sl in chunks:
            y = x_ref[:, sl] * inv * g_ref[:, sl]
            y3 = y.reshape(tm // g, g, LANES) * (1.0 + sc_ref[:, sl])[None] + sh_ref[:, sl][None]
            o_ref[:, sl] = y3.reshape(tm, LANES).astype(BF16)

    return pl.pallas_call(
        body, grid=(m // tm,),
        in_specs=[pl.BlockSpec((tm, d), lambda i: (i, 0)),
                  pl.BlockSpec((None, 1, d), lambda i: (gidx, 0, 0)),
                  pl.BlockSpec((g, d), lambda i: (rb(i), 3 * sub)),
                  pl.BlockSpec((g, d), lambda i: (rb(i), 3 * sub + 1))],
        out_specs=pl.BlockSpec((tm, d), lambda i: (i, 0)),
        out_shape=jax.ShapeDtypeStruct((m, d), BF16),
        compiler_params=_cp("parallel"))(x, norm_g3, mod, mod)


def _layernorm_silu(y, ln_g, ln_b, lead):
    m, c = y.shape
    tm = _pick(m, 256, 8)

    def body(y_ref, g_ref, b_ref, o_ref):
        v = y_ref[...]
        vc = v - jnp.mean(v, axis=-1, keepdims=True)
        n = vc * lax.rsqrt(jnp.mean(vc * vc, axis=-1, keepdims=True) + EPS) * g_ref[...] + b_ref[...]
        o_ref[...] = _silu(n).astype(BF16)

    return pl.pallas_call(
        body, grid=(m // tm,),
        in_specs=[pl.BlockSpec((tm, c), lambda i: (i, 0)),
                  pl.BlockSpec((None, 1, c), lambda i: (lead, 0, 0)),
                  pl.BlockSpec((None, 1, c), lambda i: (lead, 0, 0))],
        out_specs=pl.BlockSpec((tm, c), lambda i: (i, 0)),
        out_shape=jax.ShapeDtypeStruct((m, c), BF16),
        compiler_params=_cp("parallel"))(y, ln_g, ln_b)


def _dwconv(x, init, w, w_lead, *, n_ch, row_off_blocks, batch, rows, tstride, tt, tc, post, out_dtype):
    taps = w.shape[-2]
    halo = (taps - 1) * tstride
    hp = init.shape[1]
    nt = rows // tt
    ncb = n_ch // tc

    def body(x_ref, init_ref, w_ref, o_ref, ext_ref):
        t = pl.program_id(2)

        @pl.when(t == 0)
        def _():
            ext_ref[0:hp, :] = init_ref[...]

        @pl.when(t > 0)
        def _():
            ext_ref[0:hp, :] = ext_ref[tt:tt + hp, :]

        ext_ref[hp:hp + tt, :] = x_ref[...]
        wv = w_ref[...]
        acc = None
        if tstride == 1 and taps > 8:
            for r_ in range(8):
                n_m = (taps - 1 - r_) // 8 + 1
                o = hp - halo + r_
                win = ext_ref[o:o + tt + 8 * (n_m - 1), :]
                for m_ in range(n_m):
                    term = win[8 * m_:8 * m_ + tt, :] * wv[r_ + 8 * m_:r_ + 8 * m_ + 1, :]
                    acc = term if acc is None else acc + term
        else:
            for j in range(taps):
                o = hp - halo + j * tstride
                term = ext_ref[o:o + tt, :] * wv[j:j + 1, :]
                acc = term if acc is None else acc + term
        o_ref[...] = post(acc, pl.program_id(1)).astype(out_dtype)

    return pl.pallas_call(
        body, grid=(batch, ncb, nt),
        in_specs=[pl.BlockSpec((tt, tc), lambda b, c, t: (row_off_blocks + b * nt + t, c)),
                  pl.BlockSpec((None, hp, tc), lambda b, c, t: (b, 0, c)),
                  pl.BlockSpec((None,) * len(w_lead) + (taps, tc), lambda b, c, t: w_lead + (0, c))],
        out_specs=pl.BlockSpec((tt, tc), lambda b, c, t: (b * nt + t, c)),
        out_shape=jax.ShapeDtypeStruct((batch * rows, n_ch), out_dtype),
        scratch_shapes=[pltpu.VMEM((hp + tt, tc), F32)],
        compiler_params=_cp("parallel", "parallel", "arbitrary"))(x, init, w)


def _dwconv_both(cfg, x, w, w_lead, n_ch, init_prompt, init_sample, post, out_dtype, tt_p, tc):
    tt_p = _pick(cfg.seq, tt_p, 8)
    tc = _pick(n_ch, tc)
    ms = cfg.dseq * cfg.dbatch
    yp = _dwconv(x, init_prompt, w, w_lead, n_ch=n_ch, row_off_blocks=0, batch=cfg.batch, rows=cfg.seq,
                 tstride=1, tt=tt_p, tc=tc, post=post, out_dtype=out_dtype)
    assert cfg.m_prompt % ms == 0
    ys = _dwconv(x, init_sample, w, w_lead, n_ch=n_ch, row_off_blocks=cfg.m_prompt // ms, batch=1, rows=ms,
                 tstride=cfg.dbatch, tt=ms, tc=tc, post=post, out_dtype=out_dtype)
    return yp, ys


def _halo_inits(cfg, state, taps, n_ch):
    h = taps - 1
    hp_p = -(-h // 8) * 8
    init_p = jnp.zeros((cfg.batch, hp_p, n_ch), F32)
    init_s = jnp.swapaxes(state.astype(F32), 0, 1).reshape(1, h * cfg.dbatch, n_ch)
    return init_p, init_s


def _delta_rule(qkv, z, ba, s0, s0_lead, gparams, norm_g, *, heads, seg, n_states, steps, hpb, state_out,
                prev_states=None, name="delta_rule"):
    rows = qkv.shape[0]
    r = LANES
    nseg = r // seg
    assert n_states in (1, nseg) and (n_states == 1 or steps == 1) and (n_states == nseg or nseg == 1)
    assert heads % hpb == 0
    groups = rows // (r * steps)
    z_arr, z_off = z
    ns = n_states
    n_layers, layer = state_out
    aliased = prev_states is not None

    def body(*refs):
        q_ref, k_ref, v_ref, z_ref, ba_ref, s0_ref, gp_ref, ng_ref = refs[:8]
        o_ref, sf_ref, s_ref = refs[-3:]
        hblk = pl.program_id(1)
        ci = pl.program_id(2)

        @pl.when(ci == 0)
        def _():
            for hi in range(hpb):
                s_ref[hi * ns:(hi + 1) * ns] = s0_ref[:, hi].astype(F32)

        bav = ba_ref[...]
        gp = gp_ref[...]
        lane = lax.broadcasted_iota(jnp.int32, (r, LANES), 1)
        beta_all = jax.nn.sigmoid(bav)
        xs = bav + gp[1:2, :]
        softplus = jnp.maximum(xs, 0.0) + jnp.log1p(jnp.exp(-jnp.abs(xs)))
        g_all = -jnp.exp(gp[0:1, :]) * softplus
        row = lax.broadcasted_iota(jnp.int32, (r, r), 0)
        col = lax.broadcasted_iota(jnp.int32, (r, r), 1)
        same = (row // seg) == (col // seg)
        incl = jnp.logical_and(same, row >= col)
        strict = jnp.logical_and(same, row > col)
        last = jnp.logical_and(same, col % seg == seg - 1)
        eye = jnp.where(row == col, 1.0, 0.0)
        offs = []
        s = 1
        while s < seg:
            rb = row // s
            offs.append(jnp.logical_and(rb % 2 == 1, (col // s) == rb - 1))
            s *= 2
        rseg = lax.broadcasted_iota(jnp.int32, (r, LANES), 0) // seg
        ng = ng_ref[...]

        hs = range(hpb)
        ls = [slice(hi * LANES, (hi + 1) * LANES) for hi in hs]
        q = [q_ref[:, ls[hi]] for hi in hs]
        k = [k_ref[:, ls[hi]] for hi in hs]
        v = [v_ref[:, ls[hi]] for hi in hs]
        beta = [jnp.sum(jnp.where(lane == hblk * hpb + hi, beta_all, 0.0), axis=-1, keepdims=True) for hi in hs]
        g = [jnp.sum(jnp.where(lane == hblk * hpb + hi + heads, g_all, 0.0), axis=-1, keepdims=True) for hi in hs]
        gc = [jnp.sum(jnp.where(incl, jnp.broadcast_to(g[hi], (r, r)).T, 0.0), axis=-1, keepdims=True)
              for hi in hs]
        gi = [jnp.broadcast_to(gc[hi], (r, r)) for hi in hs]
        gj = [gi[hi].T for hi in hs]
        decay = [jnp.where(incl, jnp.exp(jnp.where(incl, gi[hi] - gj[hi], 0.0)), 0.0) for hi in hs]
        kb = [k[hi] * beta[hi] for hi in hs]
        kk = [_bdot_nt(kb[hi], k[hi]) for hi in hs]
        m_mat = [jnp.where(strict, kk[hi] * decay[hi], 0.0) for hi in hs]
        t_inv = [eye - jnp.where(offs[0], m_mat[hi], 0.0) for hi in hs] if offs else [eye for _ in hs]
        for off in offs[1:]:
            left = [_bdot(t_inv[hi], jnp.where(off, m_mat[hi], 0.0)) for hi in hs]
            corr = [_bdot(left[hi], t_inv[hi]) for hi in hs]
            t_inv = [t_inv[hi] - corr[hi] for hi in hs]
        u = [_bdot(t_inv[hi], v[hi] * beta[hi]) for hi in hs]
        w = [_bdot(t_inv[hi], kb[hi] * jnp.exp(gi[hi])) for hi in hs]
        attn = [_bdot_nt(q[hi], k[hi]) * decay[hi] for hi in hs]
        qg = [q[hi] * jnp.exp(gi[hi]) for hi in hs]
        gl = [jnp.sum(jnp.where(last, gj[hi], 0.0), axis=-1, keepdims=True) for hi in hs]
        kd_t = [(k[hi] * jnp.exp(gl[hi] - gc[hi])).T for hi in hs]
        egl = [jnp.exp(gl[hi]) for hi in hs]
        if ns == 1:
            st = [s_ref[hi] for hi in hs]
            ws = [_bdot(w[hi], st[hi]) for hi in hs]
            dlt = [u[hi] - ws[hi] for hi in hs]
            qs = [_bdot(qg[hi], st[hi]) for hi in hs]
            ad = [_bdot(attn[hi], dlt[hi]) for hi in hs]
            o = [qs[hi] + ad[hi] for hi in hs]
            upd = [_bdot(kd_t[hi], dlt[hi]) for hi in hs]
            for hi in hs:
                s_ref[hi] = st[hi] * egl[hi][r - 1:r, :] + upd[hi]
        else:
            segs = range(nseg)
            wq = [[_bdot(jnp.concatenate([w[hi][si * seg:(si + 1) * seg], qg[hi][si * seg:(si + 1) * seg]],
                                         axis=0), s_ref[hi * ns + si]) for si in segs] for hi in hs]
            dlt = [u[hi] - jnp.concatenate([x[:seg] for x in wq[hi]], axis=0) for hi in hs]
            ad = [_bdot(attn[hi], dlt[hi]) for hi in hs]
            o = [jnp.concatenate([x[seg:] for x in wq[hi]], axis=0) + ad[hi] for hi in hs]
            for hi in hs:
                for si in segs:
                    upd = _bdot(kd_t[hi], jnp.where(rseg == si, dlt[hi], 0.0))
                    s_ref[hi * ns + si] = s_ref[hi * ns + si] * egl[hi][si * seg:si * seg + 1, :] + upd
        for hi in hs:
            on = o[hi] * lax.rsqrt(jnp.mean(o[hi] * o[hi], axis=-1, keepdims=True) + EPS) * ng
            o_ref[:, ls[hi]] = (on * _silu(z_ref[:, ls[hi]])).astype(BF16)

        @pl.when(ci == steps - 1)
        def _():
            for hi in range(hpb):
                sf_ref[:, hi] = s_ref[hi * ns:(hi + 1) * ns]

    hh = heads // hpb
    wl = hpb * LANES
    zb = z_off // hpb
    assert z_off % hpb == 0
    in_specs = [pl.BlockSpec((r, wl), lambda b, h, ci: (b * steps + ci, h)),
                pl.BlockSpec((r, wl), lambda b, h, ci: (b * steps + ci, hh + h)),
                pl.BlockSpec((r, wl), lambda b, h, ci: (b * steps + ci, 2 * hh + h)),
                pl.BlockSpec((r, wl), lambda b, h, ci: (b * steps + ci, zb + h)),
                pl.BlockSpec((r, LANES), lambda b, h, ci: (b * steps + ci, 0)),
                pl.BlockSpec((None,) * len(s0_lead) + (ns, hpb, LANES, LANES),
                             lambda b, h, ci: s0_lead + (b, h, 0, 0)),
                pl.BlockSpec((8, LANES), lambda b, h, ci: (0, 0)),
                pl.BlockSpec((1, LANES), lambda b, h, ci: (0, 0))]
    args = [qkv, qkv, qkv, z_arr, ba, s0, gparams, norm_g]
    aliases = {}
    if aliased:
        in_specs.append(pl.BlockSpec(memory_space=pl.ANY))
        args.append(prev_states)
        aliases = {len(args) - 1: 1}
    return pl.pallas_call(
        body, grid=(groups, heads // hpb, steps), in_specs=in_specs,
        out_specs=[pl.BlockSpec((r, wl), lambda b, h, ci: (b * steps + ci, h)),
                   pl.BlockSpec((None, ns, hpb, LANES, LANES), lambda b, h, ci: (layer, b, h, 0, 0))],
        out_shape=[jax.ShapeDtypeStruct((rows, heads * LANES), BF16),
                   jax.ShapeDtypeStruct((n_layers, groups * ns, heads, LANES, LANES), F32)],
        scratch_shapes=[pltpu.VMEM((hpb * ns, LANES, LANES), F32)],
        input_output_aliases=aliases, name=name,
        compiler_params=_cp("parallel", "parallel", "arbitrary"))(*args)


def _rope_tables(cfg):
    half = cfg.rope // 2
    inv = ROPE_THETA ** (-jnp.arange(0, cfg.rope, 2, dtype=F32) / cfg.rope)
    pos = jnp.concatenate([jnp.tile(jnp.arange(cfg.seq), cfg.batch),
                           jnp.repeat(cfg.n_pages * cfg.page + jnp.arange(cfg.dseq), cfg.dbatch)])
    ang = pos.astype(F32)[:, None] * inv[None, :]
    cos, sin = jnp.cos(ang), jnp.sin(ang)
    zpad = jnp.zeros((cfg.m, LANES - cfg.rope), F32)
    zh = jnp.zeros((cfg.m, half), F32)
    cosf = jnp.concatenate([cos, cos, zpad], axis=1)
    sin_a = jnp.concatenate([-sin, zh, zpad], axis=1)
    sin_b = jnp.concatenate([zh, sin, zpad], axis=1)
    return cosf, sin_a, sin_b


def _rope128(x, cosf, sin_a, sin_b, half):
    return (x * cosf + pltpu.roll(x, LANES - half, 1) * sin_a + pltpu.roll(x, half, 1) * sin_b)


def _mla_cq(cfg, h, w_dq, qa_g3, i):
    m = h.shape[0]
    n = cfg.q_lora
    tm = _pick(m, 1024, 8)
    tk = _pick(cfg.d, 512)

    def ep(accs, e_refs, o_refs):
        y = accs[0]
        y = y * lax.rsqrt(jnp.mean(y * y, axis=-1, keepdims=True) + EPS) * e_refs[0][...]
        o_refs[0][...] = y.astype(BF16)

    return _mm(h, [(w_dq, (i,), 0)], grid_m=m // tm, n_blocks=1, tm=tm, tn=n, tk=tk,
               outs=[((m, n), BF16, (tm, n), lambda i_, j: (i_, 0))],
               extras=[(qa_g3, (None, 1, n), lambda i_, j: (i, 0, 0))], epilogue=ep)[0]


def _mla_q(cfg, cq, w_uq_p, gq, gk, tables):
    m = cq.shape[0]
    hw = 2 * LANES
    tm = _pick(m, 1024, 8)
    qk_dim = cfg.nope + cfg.rope
    half = cfg.rope // 2
    scale = qk_dim ** -0.5

    def ep(accs, e_refs, o_refs):
        cosf, sa, sb, gq_r, gk_r = [r[...] for r in e_refs]
        acc = accs[0]
        rot = _rope128(acc[:, LANES:], cosf, sa, sb, half)
        qf = jnp.concatenate([acc[:, :LANES], rot], axis=-1)
        ssq = jnp.sum(qf * qf, axis=-1, keepdims=True) * (1.0 / qk_dim)
        o_refs[0][...] = (qf * lax.rsqrt(ssq + EPS) * gq_r * (gk_r * scale)).astype(BF16)

    tab = [(t, (tm, LANES), lambda i, j: (i, 0)) for t in tables]
    gains = [(g, (1, hw), lambda i, j: (0, 0)) for g in (gq, gk)]
    return _mm(cq, [(w_uq_p, (), 0)], grid_m=m // tm, n_blocks=cfg.heads, tm=tm, tn=hw, tk=cfg.q_lora,
               outs=[((m, cfg.heads * hw), BF16, (tm, hw), lambda i, j: (i, j))],
               extras=tab + gains, epilogue=ep)[0]


def _mla_ckv(cfg, h, w_dkv_p, kva_g3, i, tables):
    m = h.shape[0]
    kv = cfg.kv_lora
    n = kv + LANES
    tm = _pick(m, 512, 8)
    tk = _pick(cfg.d, 512)
    half = cfg.rope // 2

    def ep(accs, e_refs, o_refs):
        cosf, sa, sb, g = [r[...] for r in e_refs]
        acc = accs[0]
        c = acc[:, :kv]
        o_refs[0][...] = c * lax.rsqrt(jnp.mean(c * c, axis=-1, keepdims=True) + EPS) * g
        o_refs[1][...] = _rope128(acc[:, kv:], cosf, sa, sb, half)

    tab = [(t, (tm, LANES), lambda i_, j: (i_, 0)) for t in tables]
    return _mm(h, [(w_dkv_p, (), 0)], grid_m=m // tm, n_blocks=1, tm=tm, tn=n, tk=tk,
               outs=[((m, kv), F32, (tm, kv), lambda i_, j: (i_, 0)),
                     ((m, LANES), F32, (tm, LANES), lambda i_, j: (i_, 0))],
               extras=tab + [(kva_g3, (None, 1, kv), lambda i_, j: (i, 0, 0))], epilogue=ep)


def _mla_prompt_kv(cfg, ckv, krope_p, w_ukv, i):
    hw = 2 * LANES
    mp = cfg.m_prompt
    tm = _pick(mp, 1024, 8)
    qk_dim = cfg.nope + cfg.rope

    def ep(accs, e_refs, o_refs):
        acc = accs[0]
        kr = e_refs[0][...]
        kn = acc[:, :LANES]
        ssq = (jnp.sum(kn * kn, axis=-1, keepdims=True) + jnp.sum(kr * kr, axis=-1, keepdims=True))
        s = lax.rsqrt(ssq * (1.0 / qk_dim) + EPS)
        o_refs[0][...] = jnp.concatenate([kn * s, kr * s], axis=-1).astype(BF16)
        o_refs[1][...] = acc[:, LANES:].astype(BF16)

    return _mm(ckv, [(w_ukv, (i,), 0)], grid_m=mp // tm, n_blocks=cfg.heads, tm=tm, tn=hw, tk=cfg.kv_lora,
               outs=[((mp, cfg.heads * hw), BF16, (tm, hw), lambda i_, j: (i_, j)),
                     ((mp, cfg.heads * LANES), BF16, (tm, LANES), lambda i_, j: (i_, j))],
               extras=[(krope_p, (tm, LANES), lambda i_, j: (i_, 0))], epilogue=ep)


def _flash_prompt(cfg, q, k, v):
    tq = _pick(cfg.seq, 512, 8)
    nq = cfg.seq // tq
    hw = 2 * LANES
    hpb = _pick(cfg.heads, 4, 1)
    hs = range(hpb)

    def body(q_ref, k_ref, v_ref, o_ref, m_ref, l_ref, acc_ref):
        qi = pl.program_id(2)
        kj = pl.program_id(3)

        @pl.when(kj == 0)
        def _():
            m_ref[...] = jnp.full(m_ref.shape, NEG, F32)
            l_ref[...] = jnp.zeros(l_ref.shape, F32)
            acc_ref[...] = jnp.zeros(acc_ref.shape, F32)

        @pl.when(kj <= qi)
        def _():
            qpos = qi * tq + lax.broadcasted_iota(jnp.int32, (tq, tq), 0)
            kpos = kj * tq + lax.broadcasted_iota(jnp.int32, (tq, tq), 1)
            visible = kpos <= qpos
            s = [lax.dot_general(q_ref[:, h * hw:(h + 1) * hw], k_ref[:, h * hw:(h + 1) * hw],
                                 (((1,), (1,)), ((), ())), preferred_element_type=F32) for h in hs]
            s = [jnp.where(visible, s[h], NEG) for h in hs]
            m_prev = [m_ref[h] for h in hs]
            m_new = [jnp.maximum(m_prev[h], jnp.max(s[h], axis=-1, keepdims=True)) for h in hs]
            alpha = [jnp.exp(m_prev[h] - m_new[h]) for h in hs]
            p = [jnp.exp(s[h] - m_new[h]) for h in hs]
            pv = [jnp.dot(p[h].astype(BF16), v_ref[:, h * LANES:(h + 1) * LANES], preferred_element_type=F32)
                  for h in hs]
            for h in hs:
                l_ref[h] = alpha[h] * l_ref[h] + jnp.sum(p[h], axis=-1, keepdims=True)
                acc_ref[h] = alpha[h] * acc_ref[h] + pv[h]
                m_ref[h] = m_new[h]

        @pl.when(kj == nq - 1)
        def _():
            for h in hs:
                o_ref[:, h * LANES:(h + 1) * LANES] = (acc_ref[h] / l_ref[h]).astype(BF16)

    return pl.pallas_call(
        body, grid=(cfg.batch, cfg.heads // hpb, nq, nq),
        in_specs=[pl.BlockSpec((tq, hpb * hw), lambda b, h, qi, kj: (b * nq + qi, h)),
                  pl.BlockSpec((tq, hpb * hw), lambda b, h, qi, kj: (b * nq + jnp.minimum(kj, qi), h)),
                  pl.BlockSpec((tq, hpb * LANES), lambda b, h, qi, kj: (b * nq + jnp.minimum(kj, qi), h))],
        out_specs=pl.BlockSpec((tq, hpb * LANES), lambda b, h, qi, kj: (b * nq + qi, h)),
        out_shape=jax.ShapeDtypeStruct((cfg.m_prompt, cfg.heads * LANES), BF16),
        scratch_shapes=[pltpu.VMEM((hpb, tq, 1), F32), pltpu.VMEM((hpb, tq, 1), F32),
                        pltpu.VMEM((hpb, tq, LANES), F32)],
        name="flash_prompt", compiler_params=_cp("parallel", "parallel", "parallel", "arbitrary"))(q, k, v)


def _head_mm(a, w, *, a_block_of_head, kdim, n, out_dtype):
    rows = a.shape[0]
    heads = w.shape[0]

    def body(a_ref, w_ref, o_ref):
        o_ref[...] = jnp.dot(a_ref[...].astype(BF16), w_ref[...], preferred_element_type=F32).astype(out_dtype)

    return pl.pallas_call(
        body, grid=(heads,),
        in_specs=[pl.BlockSpec((rows, kdim), lambda h: (0, a_block_of_head(h))),
                  pl.BlockSpec((None, kdim, n), lambda h: (h, 0, 0))],
        out_specs=pl.BlockSpec((rows, n), lambda h: (0, h)),
        out_shape=jax.ShapeDtypeStruct((rows, heads * n), out_dtype),
        compiler_params=_cp("parallel"))(a, w)


def _decode_attention(cfg, page_table, cache_ckv, cache_krope_t, layer, qabs, qr, c_new, kr_new_t, w_uk_t):
    db, hq, kv = qabs.shape
    tq = cfg.dseq
    heads = cfg.heads
    npg = cfg.n_pages
    ppk = _pick(npg, 8, 2)
    assert npg % ppk == 0 and ppk % 2 == 0 and hq == heads * tq and tq == 8
    nsteps = npg // ppk
    page = cfg.page
    sub = 2 * page
    qk_dim = cfg.nope + cfg.rope
    hb_rows = min(heads, 8) * cfg.nope

    def body(pt_ref, *refs):
        c_refs = refs[:ppk]
        r_refs = refs[ppk:2 * ppk]
        qa_ref, qr_ref, cn_ref, rn_ref, wt_ref, o_ref, m_ref, l_ref, acc_ref, s_ref = refs[2 * ppk:]
        p = pl.program_id(1)

        @pl.when(p == 0)
        def _():
            m_ref[...] = jnp.full(m_ref.shape, NEG, F32)
            l_ref[...] = jnp.zeros(l_ref.shape, F32)
            acc_ref[...] = jnp.zeros(acc_ref.shape, F32)

        def process(cb, kr_t, new_rows):
            nkeys = cb.shape[0]
            ssq_r = jnp.sum(kr_t * kr_t, axis=0, keepdims=True)
            s_all = _bdot_nt(qa_ref[...], cb) + _bdot(qr_ref[...], kr_t)
            for lo in range(0, nkeys, sub):
                hi_ = min(lo + sub, nkeys)
                for hb in range(heads * cfg.nope // hb_rows):
                    kt = _bdot_nt(wt_ref[hb * hb_rows:(hb + 1) * hb_rows, :], cb[lo:hi_])
                    for hh in range(hb_rows // cfg.nope):
                        hd = hb * (hb_rows // cfg.nope) + hh
                        sl = kt[hh * cfg.nope:(hh + 1) * cfg.nope, :]
                        ssq = jnp.sum(sl * sl, axis=0, keepdims=True) + ssq_r[:, lo:hi_]
                        ksc = lax.rsqrt(ssq * (1.0 / qk_dim) + EPS)
                        s_ref[hd * tq:(hd + 1) * tq, lo:hi_] = s_all[hd * tq:(hd + 1) * tq, lo:hi_] * ksc
            s = s_ref[:, 0:nkeys]
            if new_rows:
                qrow = lax.broadcasted_iota(jnp.int32, (hq, nkeys), 0) % tq
                kcol = lax.broadcasted_iota(jnp.int32, (hq, nkeys), 1)
                s = jnp.where(kcol <= qrow, s, NEG)
            m_prev = m_ref[...]
            m_new = jnp.maximum(m_prev, jnp.max(s, axis=-1, keepdims=True))
            alpha = jnp.exp(m_prev - m_new)
            pr = jnp.exp(s - m_new)
            l_ref[...] = alpha * l_ref[...] + jnp.sum(pr, axis=-1, keepdims=True)
            acc_ref[...] = alpha * acc_ref[...] + jnp.dot(pr.astype(BF16), cb, preferred_element_type=F32)
            m_ref[...] = m_new

        @pl.when(p < nsteps)
        def _():
            process(jnp.concatenate([c[...].astype(BF16) for c in c_refs], axis=0),
                    jnp.concatenate([r_[...] for r_ in r_refs], axis=1), False)

        @pl.when(p == nsteps)
        def _():
            process(cn_ref[...].astype(BF16), rn_ref[...], True)
            o_ref[...] = acc_ref[...] / l_ref[...]

    def pg(which):
        def imap(b, p, pt):
            pp = jnp.minimum(p, nsteps - 1)
            return (layer, pt[b * npg + ppk * pp + which], 0, 0)
        return imap

    grid_spec = pltpu.PrefetchScalarGridSpec(
        num_scalar_prefetch=1, grid=(db, nsteps + 1),
        in_specs=[pl.BlockSpec((None, None, page, kv), pg(w_)) for w_ in range(ppk)]
        + [pl.BlockSpec((None, None, cfg.rope, page), pg(w_)) for w_ in range(ppk)]
        + [pl.BlockSpec((None, hq, kv), lambda b, p, pt: (b, 0, 0)),
                  pl.BlockSpec((None, hq, cfg.rope), lambda b, p, pt: (b, 0, 0)),
                  pl.BlockSpec((None, LANES, kv), lambda b, p, pt: (b, 0, 0)),
                  pl.BlockSpec((None, cfg.rope, LANES), lambda b, p, pt: (b, 0, 0)),
                  pl.BlockSpec((heads * cfg.nope, kv), lambda b, p, pt: (0, 0))],
        out_specs=pl.BlockSpec((None, hq, kv), lambda b, p, pt: (b, 0, 0)),
        scratch_shapes=[pltpu.VMEM((hq, 1), F32), pltpu.VMEM((hq, 1), F32), pltpu.VMEM((hq, kv), F32),
                        pltpu.VMEM((hq, ppk * page), F32)])
    return pl.pallas_call(
        body, grid_spec=grid_spec, out_shape=jax.ShapeDtypeStruct((db, hq, kv), F32),
        name="decode_attention", compiler_params=_cp("parallel", "arbitrary"))(
            page_table.reshape(-1), *([cache_ckv] * ppk), *([cache_krope_t] * ppk),
            qabs, qr, c_new, kr_new_t, w_uk_t)


def _to_batch_major(cfg, rows):
    return jnp.swapaxes(rows.reshape(cfg.dseq, cfg.dbatch, rows.shape[-1]), 0, 1)


def _to_time_major(cfg, arr):
    return jnp.swapaxes(arr, 0, 1).reshape(cfg.dseq * cfg.dbatch, arr.shape[-1])


def _last_rows_prompt(cfg, rows, n_last, n_ch):
    return jnp.stack([lax.slice(rows, ((b + 1) * cfg.seq - n_last, 0), ((b + 1) * cfg.seq, n_ch))
                      for b in range(cfg.batch)])


def _last_rows_sample(cfg, rows, state, n_last, n_ch):
    mp, g, t = cfg.m_prompt, cfg.dbatch, cfg.dseq
    keep = max(n_last - t, 0)
    new = lax.slice(rows, (mp + max(t - n_last, 0) * g, 0), (mp + t * g, n_ch))
    new = jnp.swapaxes(new.reshape(-1, g, n_ch), 0, 1)
    if keep == 0:
        return new
    return jnp.concatenate([state[:, n_last - keep:].astype(F32), new], axis=1)


def _ffn(cfg, x, mod, norm_g3, l, sub, which, wg, wu, wd):
    h = _adanorm(cfg, x, norm_g3, 3 * l + sub, mod, sub)
    m, d = x.shape
    dff = wg.shape[-1]
    tm = _pick(m, 1024, 8)
    tn = _pick(dff, 256)
    tk = _pick(d, 4096)
    act = _mm(h, [(wg, (l, which), 0), (wu, (l, which), 0)], grid_m=m // tm, n_blocks=dff // tn,
              tm=tm, tn=tn, tk=tk, outs=[((m, dff), BF16, (tm, tn), lambda i, j: (i, j))],
              epilogue=_ep_swiglu, name="ffn_up")[0]
    return _resid_mm(cfg, act, wd, (l, which), x, mod, 3 * sub + 2, 0.5, name="ffn_down")


def _deltanet_layer(cfg, h, i, state_delta, state_delta_conv, dn_w_in, dn_conv_w, dn_a_log, dn_dt_bias,
                    dn_norm_g, prompt_chunk, prev_p, prev_s):
    heads, dk = cfg.heads, cfg.dk
    d_a = heads * dk
    n_qkvz = 4 * d_a
    mp = cfg.m_prompt
    w_in_t = jnp.swapaxes(dn_w_in, 1, 2)
    proj = _plain_mm(cfg, h, w_in_t, (i,), n=n_qkvz, w_nt=True, name="dn_in_proj")
    w_ba_t = jnp.pad(w_in_t[i, n_qkvz:, :], ((0, LANES - 2 * heads), (0, 0)))
    ba = _plain_mm(cfg, h, w_ba_t, (), n=LANES, tn=LANES, w_nt=True, name="dn_ba_proj")

    nq = d_a
    tc = _pick(3 * d_a, 512)

    def post(y, cb):
        y = _silu(y)
        is_qk = cb < (2 * nq) // tc
        is_q = cb < nq // tc
        outs = []
        for s in range(tc // dk):
            ys = y[:, s * dk:(s + 1) * dk]
            ssq = jnp.sum(ys * ys, axis=-1, keepdims=True)
            sc = jnp.where(is_qk, lax.rsqrt(ssq + EPS), 1.0) * jnp.where(is_q, dk ** -0.5, 1.0)
            outs.append(ys * sc)
        return jnp.concatenate(outs, axis=-1)

    taps = dn_conv_w.shape[1]
    init_p, init_s = _halo_inits(cfg, state_delta_conv[i], taps, 3 * d_a)
    qkv_p, qkv_s = _dwconv_both(cfg, proj, dn_conv_w, (i,), 3 * d_a, init_p, init_s, post, F32, 256, tc)

    gparams = jnp.zeros((8, LANES), F32)
    gparams = gparams.at[0, heads:2 * heads].set(dn_a_log[i].astype(F32))
    gparams = gparams.at[1, heads:2 * heads].set(dn_dt_bias[i].astype(F32))
    ng = dn_norm_g[i].astype(F32).reshape(1, dk)

    s0_p = jnp.zeros((cfg.batch, heads, dk, dk), F32)
    n_layers = state_delta.shape[0]
    o_p, s_p = _delta_rule(qkv_p, (proj, 3 * heads), ba, s0_p, (), gparams, ng,
                           heads=heads, seg=prompt_chunk, n_states=1, steps=cfg.seq // LANES,
                           hpb=_pick(heads, 8, 1), state_out=(n_layers, i), prev_states=prev_p,
                           name="delta_rule_prompt")
    ms = cfg.dseq * cfg.dbatch
    o_s, s_s = _delta_rule(_to_batch_major(cfg, qkv_s).reshape(ms, 3 * d_a),
                           (_to_batch_major(cfg, proj[mp:, 3 * d_a:]).reshape(ms, d_a), 0),
                           _to_batch_major(cfg, ba[mp:]).reshape(ms, LANES), state_delta, (i,), gparams, ng,
                           heads=heads, seg=cfg.dseq, n_states=LANES // cfg.dseq, steps=1,
                           hpb=_pick(heads, 2, 1), state_out=(n_layers, i), prev_states=prev_s,
                           name="delta_rule_sample")
    o_all = jnp.concatenate([o_p, _to_time_major(cfg, o_s.reshape(cfg.dbatch, cfg.dseq, d_a))], axis=0)
    new_conv_p = _last_rows_prompt(cfg, proj, taps - 1, 3 * d_a)
    new_conv_s = _last_rows_sample(cfg, proj, state_delta_conv[i], taps - 1, 3 * d_a)
    return o_all, s_p, s_s, new_conv_p, new_conv_s


def _conformer_layer(cfg, h, i, state_conv, cv_w_pw1, cv_b_pw1, cv_w_dw, cv_b_dw, cv_ln_g, cv_ln_b):
    m, d = h.shape
    dc = cv_w_dw.shape[-1]
    tm = _pick(m, 1024, 8)
    tn = _pick(dc, 256)
    tk = _pick(d, 4096)
    b3 = cv_b_pw1.reshape(cv_b_pw1.shape[0], 1, 2 * dc)
    u = _mm(h, [(cv_w_pw1, (i,), 0), (cv_w_pw1, (i,), dc // tn)], grid_m=m // tm, n_blocks=dc // tn,
            tm=tm, tn=tn, tk=tk, outs=[((m, dc), F32, (tm, tn), lambda i_, j: (i_, j))],
            extras=[(b3, (None, 1, tn), lambda i_, j: (i, 0, j)),
                    (b3, (None, 1, tn), lambda i_, j: (i, 0, dc // tn + j))],
            epilogue=_ep_glu, name="conformer_glu")[0]
    taps = cv_w_dw.shape[1]
    init_p, init_s = _halo_inits(cfg, state_conv[i], taps, dc)
    yp, ys = _dwconv_both(cfg, u, cv_w_dw, (i,), dc, init_p, init_s, lambda y, cb: y, F32, 128, 256)
    y = jnp.concatenate([yp, ys], axis=0) + cv_b_dw[i][None, :]
    hn = _layernorm_silu(y, cv_ln_g.reshape(-1, 1, dc), cv_ln_b.reshape(-1, 1, dc), i)
    new_p = _last_rows_prompt(cfg, u, taps - 1, dc)
    new_s = _last_rows_sample(cfg, u, state_conv[i], taps - 1, dc)
    return hn, new_p, new_s


def _mla_layer(cfg, h, i, cache_ckv, cache_krope, page_table, tables, m_w_dq, m_qa_g, m_w_uq, m_w_dkv,
               m_kva_g, m_w_ukv, m_q_norm_g, m_k_norm_g):
    heads, nope, rope, kv = cfg.heads, cfg.nope, cfg.rope, cfg.kv_lora
    qk_dim = nope + rope
    hw = 2 * LANES
    mp = cfg.m_prompt
    half = rope // 2

    def tied(g):
        g = g.astype(F32)
        return jnp.concatenate([g[:nope], g[nope:], g[nope:], jnp.zeros((hw - qk_dim,), F32)]).reshape(1, hw)

    w_uq_p = jnp.pad(m_w_uq[i].reshape(cfg.q_lora, heads, qk_dim),
                     ((0, 0), (0, 0), (0, hw - qk_dim))).reshape(cfg.q_lora, heads * hw)
    w_dkv_p = jnp.pad(m_w_dkv[i], ((0, 0), (0, LANES - rope)))
    cq = _mla_cq(cfg, h, m_w_dq, m_qa_g.reshape(-1, 1, cfg.q_lora), i)
    q = _mla_q(cfg, cq, w_uq_p, tied(m_q_norm_g[i]), tied(m_k_norm_g[i]), tables)
    ckv, krope_p = _mla_ckv(cfg, h, w_dkv_p, m_kva_g.reshape(-1, 1, kv), i, tables)

    k_p, v_p = _mla_prompt_kv(cfg, ckv, krope_p, m_w_ukv, i)
    att_p = _flash_prompt(cfg, q, k_p, v_p)

    w3 = m_w_ukv[i].reshape(kv, heads, nope + LANES)
    w_uk_h = jnp.transpose(w3[:, :, :nope], (1, 2, 0)).astype(BF16)
    w_uv_h = jnp.transpose(w3[:, :, nope:], (1, 0, 2)).astype(BF16)
    q_s = q[mp:]
    ms = q_s.shape[0]
    qabs = _head_mm(q_s, w_uk_h, a_block_of_head=lambda hd: 2 * hd, kdim=nope, n=kv, out_dtype=BF16)
    qabs = jnp.transpose(qabs.reshape(cfg.dseq, cfg.dbatch, heads, kv), (1, 2, 0, 3)).reshape(
        cfg.dbatch, heads * cfg.dseq, kv)
    qr = q_s.reshape(cfg.dseq, cfg.dbatch, heads, hw)[..., nope:qk_dim]
    qr = jnp.transpose(qr, (1, 2, 0, 3)).reshape(cfg.dbatch, heads * cfg.dseq, rope)
    ckv_s = _to_batch_major(cfg, ckv[mp:])
    kr_s = _to_batch_major(cfg, krope_p[mp:, :rope])
    c_new = jnp.pad(ckv_s, ((0, 0), (0, LANES - cfg.dseq), (0, 0)))
    kr_new_t = jnp.pad(jnp.swapaxes(kr_s, 1, 2), ((0, 0), (0, 0), (0, LANES - cfg.dseq)))
    o_lat = _decode_attention(cfg, page_table, cache_ckv, jnp.swapaxes(cache_krope, 2, 3), i, qabs, qr, c_new,
                              kr_new_t,
                              w_uk_h.reshape(heads * nope, kv))
    o_lat = jnp.transpose(o_lat.reshape(cfg.dbatch, heads, cfg.dseq, kv), (2, 0, 1, 3)).reshape(ms, heads * kv)
    att_s = _head_mm(o_lat, w_uv_h, a_block_of_head=lambda hd: hd, kdim=kv, n=LANES, out_dtype=BF16)
    att = jnp.concatenate([att_p, att_s], axis=0)
    ckv_p_out = ckv[:mp].reshape(cfg.batch, cfg.seq, kv)
    kr_p_out = krope_p[:mp, :rope].reshape(cfg.batch, cfg.seq, rope)
    return att, ckv_p_out, kr_p_out, ckv_s, kr_s


def kernel(x_prompt, x_sample, cache_ckv, cache_krope, state_delta, state_delta_conv, state_conv, page_table,
           c_prompt, c_sample, ada_w, ada_b, norm_g, ffn_wg, ffn_wu, ffn_wd, dn_w_in, dn_conv_w, dn_a_log,
           dn_dt_bias, dn_norm_g, dn_w_out, cv_w_pw1, cv_b_pw1, cv_w_dw, cv_b_dw, cv_ln_g, cv_ln_b, cv_w_pw2,
           cv_b_pw2, m_w_dq, m_qa_g, m_w_uq, m_w_dkv, m_kva_g, m_w_ukv, m_q_norm_g, m_k_norm_g, m_w_o):
    batch, seq, d = x_prompt.shape
    dbatch, dseq, _ = x_sample.shape
    depth = ada_w.shape[0]
    rope = cache_krope.shape[-1]
    kv_lora = cache_ckv.shape[-1]
    heads = dn_a_log.shape[-1]
    nope = m_q_norm_g.shape[-1] - rope // 2
    cfg = Cfg(batch=batch, seq=seq, dbatch=dbatch, dseq=dseq, d=d, m_prompt=batch * seq,
              m=batch * seq + dbatch * dseq, n_pages=page_table.shape[1], page=cache_ckv.shape[2],
              heads=heads, dk=state_delta.shape[-2], nope=nope, rope=rope, kv_lora=kv_lora,
              q_lora=m_w_dq.shape[-1])
    assert dbatch % 8 == 0 and seq % dbatch == 0 and cfg.dk == LANES and nope == LANES

    x = jnp.concatenate([x_prompt.reshape(batch * seq, d).astype(F32),
                         jnp.swapaxes(x_sample.astype(F32), 0, 1).reshape(dseq * dbatch, d)], axis=0)
    c_all = jnp.concatenate([jnp.repeat(c_prompt, dbatch, axis=0), c_sample], axis=0).astype(F32)
    norm_g3 = norm_g.reshape(depth * 3, 1, d).astype(F32)
    ada_b3 = ada_b.reshape(depth, 1, N_ADA * d)
    tables = _rope_tables(cfg)
    mrows = c_all.shape[0]
    tn_ada = _pick(N_ADA * d, 512)
    tk_ada = _pick(d, 4096)

    def silu_body(c_ref, o_ref):
        o_ref[...] = _silu(c_ref[...]).astype(BF16)

    c_act = pl.pallas_call(silu_body, out_shape=jax.ShapeDtypeStruct(c_all.shape, BF16), name="ada_silu")(c_all)

    def ep_bias(accs, e_refs, o_refs):
        o_refs[0][...] = accs[0] + e_refs[0][...]

    outs = {k: [] for k in ("ckv_p", "kr_p", "ckv_s", "kr_s", "dnc_p", "dnc_s", "cv_p", "cv_s")}
    dn_p = dn_s = None
    for l in range(depth):
        mod = _mm(c_act, [(ada_w, (l,), 0)], grid_m=1, n_blocks=N_ADA * d // tn_ada, tm=mrows, tn=tn_ada,
                  tk=tk_ada, outs=[((mrows, N_ADA * d), F32, (mrows, tn_ada), lambda i, j: (0, j))],
                  extras=[(ada_b3, (None, 1, tn_ada), lambda i, j, l=l: (l, 0, j))], epilogue=ep_bias,
                  name="ada")[0]
        x = _ffn(cfg, x, mod, norm_g3, l, 0, 0, ffn_wg, ffn_wu, ffn_wd)
        h = _adanorm(cfg, x, norm_g3, 3 * l + 1, mod, 1)
        kind, i = l % 3, l // 3
        if kind == 0:
            o, dn_p, dn_s, nc_p, nc_s = _deltanet_layer(cfg, h, i, state_delta, state_delta_conv, dn_w_in,
                                                        dn_conv_w, dn_a_log, dn_dt_bias, dn_norm_g,
                                                        min(LANES, seq), dn_p, dn_s)
            x = _resid_mm(cfg, o, dn_w_out, (i,), x, mod, 5, 1.0)
            outs["dnc_p"].append(nc_p)
            outs["dnc_s"].append(nc_s)
        elif kind == 1:
            hn, nb_p, nb_s = _conformer_layer(cfg, h, i, state_conv, cv_w_pw1, cv_b_pw1, cv_w_dw, cv_b_dw,
                                              cv_ln_g, cv_ln_b)
            x = _resid_mm(cfg, hn, cv_w_pw2, (i,), x, mod, 5, 1.0,
                          bias=cv_b_pw2.reshape(-1, 1, d), bias_lead=(i,))
            outs["cv_p"].append(nb_p)
            outs["cv_s"].append(nb_s)
        else:
            att, ckv_p, kr_p, ckv_s, kr_s = _mla_layer(cfg, h, i, cache_ckv, cache_krope, page_table, tables,
                                                       m_w_dq, m_qa_g, m_w_uq, m_w_dkv, m_kva_g, m_w_ukv,
                                                       m_q_norm_g, m_k_norm_g)
            x = _resid_mm(cfg, att, m_w_o, (i,), x, mod, 5, 1.0)
            outs["ckv_p"].append(ckv_p)
            outs["kr_p"].append(kr_p)
            outs["ckv_s"].append(ckv_s)
            outs["kr_s"].append(kr_s)
        x = _ffn(cfg, x, mod, norm_g3, l, 2, 1, ffn_wg, ffn_wu, ffn_wd)

    mp = cfg.m_prompt
    y_prompt = x[:mp].reshape(batch, seq, d)
    y_sample = jnp.swapaxes(x[mp:].reshape(dseq, dbatch, d), 0, 1)
    st = jnp.stack
    return (y_prompt, y_sample, st(outs["ckv_p"]), st(outs["kr_p"]), st(outs["ckv_s"]), st(outs["kr_s"]),
            dn_p, dn_s, st(outs["dnc_p"]), st(outs["dnc_s"]),
            st(outs["cv_p"]), st(outs["cv_s"]))
```

```python
import collections
import functools

import jax
import jax.numpy as jnp
from jax import lax
from jax.experimental import pallas as pl
from jax.experimental.pallas import tpu as pltpu

F32 = jnp.float32
BF16 = jnp.bfloat16
EPS = 1e-6
N_ADA = 9
ROPE_THETA = 10000.0
LANES = 128
VMEM_LIMIT = 56 * 1024 * 1024
NEG = -1e30

Cfg = collections.namedtuple(
    "Cfg", "batch seq dbatch dseq d m_prompt m n_pages page heads dk nope rope kv_lora q_lora")


def _cp(*sem):
    return pltpu.CompilerParams(dimension_semantics=sem, vmem_limit_bytes=VMEM_LIMIT)


def _pick(dim, pref, mult=LANES):
    if dim <= pref:
        return dim
    t = (pref // mult) * mult
    while t >= mult:
        if dim % t == 0:
            return t
        t -= mult
    return dim


def _pick_tm(cfg, pref):
    g = cfg.dbatch
    t = (pref // g) * g
    while t > g:
        if cfg.seq % t == 0 and (cfg.dseq * g) % t == 0:
            return t
        t -= g
    return g


def _rowblk(cfg, tm):
    return lambda i: jnp.minimum((i * tm) // cfg.seq, cfg.batch)


def _bdot(a, b):
    return jnp.dot(a.astype(BF16), b.astype(BF16), preferred_element_type=F32)


def _bdot_nt(a, b):
    return lax.dot_general(a.astype(BF16), b.astype(BF16), (((1,), (1,)), ((), ())),
                           preferred_element_type=F32)


def _silu(x):
    return x * jax.nn.sigmoid(x)


def _mm(a, ws, *, grid_m, n_blocks, tm, tn, tk, outs, epilogue, extras=(), row_off=0, w_nt=False, name="mm",
        a_single=False):
    nk = a.shape[1] // tk
    nw, ne, no = len(ws), len(extras), len(outs)
    a_mode = dict(pipeline_mode=pl.Buffered(1)) if a_single else {}
    in_specs = [pl.BlockSpec((tm, tk), lambda i, j, k: (i + row_off, k), **a_mode)]
    args = [a]
    for w, lead, off in ws:
        if w_nt:
            in_specs.append(pl.BlockSpec((None,) * len(lead) + (tn, tk),
                                         lambda i, j, k, lead=lead, off=off: lead + (j + off, k)))
        else:
            in_specs.append(pl.BlockSpec((None,) * len(lead) + (tk, tn),
                                         lambda i, j, k, lead=lead, off=off: lead + (k, j + off)))
        args.append(w)

    def wdot(av, w):
        dims = (((1,), (1,)), ((), ())) if w_nt else (((1,), (0,)), ((), ()))
        return lax.dot_general(av[...].astype(BF16), w[...].astype(BF16), dims, preferred_element_type=F32)
    for arr, blk, imap in extras:
        in_specs.append(pl.BlockSpec(blk, lambda i, j, k, imap=imap: imap(i, j)))
        args.append(arr)
    out_specs = [pl.BlockSpec(blk, lambda i, j, k, imap=imap: imap(i, j)) for _, _, blk, imap in outs]
    out_shape = [jax.ShapeDtypeStruct(s, d) for s, d, _, _ in outs]

    def body(*refs):
        a_ref = refs[0]
        w_refs = refs[1:1 + nw]
        e_refs = refs[1 + nw:1 + nw + ne]
        o_refs = refs[1 + nw + ne:1 + nw + ne + no]
        acc_refs = refs[1 + nw + ne + no:]
        av = a_ref
        if nk == 1:
            epilogue([wdot(av, w) for w in w_refs], e_refs, o_refs)
            return
        k = pl.program_id(2)

        @pl.when(k == 0)
        def _():
            for acc, w in zip(acc_refs, w_refs):
                acc[...] = wdot(av, w)

        if nk > 2:
            @pl.when(jnp.logical_and(k > 0, k < nk - 1))
            def _():
                for acc, w in zip(acc_refs, w_refs):
                    acc[...] += wdot(av, w)

        @pl.when(k == nk - 1)
        def _():
            epilogue([acc[...] + wdot(av, w) for acc, w in zip(acc_refs, w_refs)], e_refs, o_refs)

    scratch = [pltpu.VMEM((tm, tn), F32) for _ in range(nw)] if nk > 1 else []
    res = pl.pallas_call(
        body, grid=(grid_m, n_blocks, nk), in_specs=in_specs, out_specs=out_specs,
        out_shape=out_shape, scratch_shapes=scratch, name=name,
        compiler_params=_cp("parallel", "parallel", "arbitrary"))(*args)
    return res


def _ep_store(accs, e_refs, o_refs):
    o_refs[0][...] = accs[0].astype(o_refs[0].dtype)


def _ep_swiglu(accs, e_refs, o_refs):
    o_refs[0][...] = (_silu(accs[0]) * accs[1]).astype(o_refs[0].dtype)


def _ep_glu(accs, e_refs, o_refs):
    a = accs[0] + e_refs[0][...]
    b = accs[1] + e_refs[1][...]
    o_refs[0][...] = (a * jax.nn.sigmoid(b)).astype(o_refs[0].dtype)


def _ep_resid(coef, has_bias, accs, e_refs, o_refs):
    y = accs[0]
    if has_bias:
        y = y + e_refs[2][...]
    g = e_refs[1][...]
    tm, tn = y.shape
    gy = (y.reshape(tm // g.shape[0], g.shape[0], tn) * g[None]).reshape(tm, tn)
    o_refs[0][...] = e_refs[0][...] + coef * gy


def _plain_mm(cfg, a, w, lead, *, n, col_off=0, out_dtype=F32, tm=1024, tn=512, tk=4096, w_nt=False,
              name="mm"):
    m = a.shape[0]
    tm = _pick(m, tm, 8)
    tn = _pick(n, tn)
    tk = _pick(a.shape[1], tk)
    return _mm(a, [(w, lead, col_off // tn)], grid_m=m // tm, n_blocks=n // tn, tm=tm, tn=tn, tk=tk,
               outs=[((m, n), out_dtype, (tm, tn), lambda i, j: (i, j))], epilogue=_ep_store, w_nt=w_nt,
               name=name)[0]


def _resid_mm(cfg, a, w, lead, x, mod, gate_idx, coef, bias=None, bias_lead=(), tm=1024, tn=512, tk=4096,
              name="resid_mm"):
    m, d = x.shape
    tm = _pick_tm(cfg, tm)
    tk = _pick(a.shape[1], tk)
    if a.shape[1] > tk and bias is None:
        return _resid_mm_ksplit(cfg, a, w, lead, x, mod, gate_idx, coef, tm, _pick(d, 512), tk, name)
    tn = _pick(d, tn)
    rb = _rowblk(cfg, tm)
    goff = gate_idx * (d // tn)
    extras = [(x, (tm, tn), lambda i, j: (i, j)),
              (mod, (cfg.dbatch, tn), lambda i, j: (rb(i), goff + j))]
    if bias is not None:
        extras.append((bias, (None,) * len(bias_lead) + (1, tn), lambda i, j: bias_lead + (0, j)))
    return _mm(a, [(w, lead, 0)], grid_m=m // tm, n_blocks=d // tn, tm=tm, tn=tn, tk=tk,
               outs=[((m, d), F32, (tm, tn), lambda i, j: (i, j))], extras=extras,
               epilogue=functools.partial(_ep_resid, coef, bias is not None), name=name)[0]


def _resid_mm_ksplit(cfg, a, w, lead, x, mod, gate_idx, coef, tm, tn, tk, name):
    m, d = x.shape
    nk = a.shape[1] // tk
    nj = d // tn
    rb = _rowblk(cfg, tm)
    goff = gate_idx * nj
    g = cfg.dbatch

    def col(j, k):
        return jnp.where(k == nk - 1, j, 0)

    def dot(a_ref, w_ref):
        return jnp.dot(a_ref[...].astype(BF16), w_ref[...].astype(BF16), preferred_element_type=F32)

    def body(a_ref, w_ref, x_ref, g_ref, o_ref, acc_ref):
        k = pl.program_id(1)
        j = pl.program_id(2)

        @pl.when(k == 0)
        def _():
            acc_ref[j] = dot(a_ref, w_ref)

        if nk > 2:
            @pl.when(jnp.logical_and(k > 0, k < nk - 1))
            def _():
                acc_ref[j] += dot(a_ref, w_ref)

        @pl.when(k == nk - 1)
        def _():
            y = acc_ref[j] + dot(a_ref, w_ref)
            gy = (y.reshape(tm // g, g, tn) * g_ref[...][None]).reshape(tm, tn)
            o_ref[...] = x_ref[...] + coef * gy

    return pl.pallas_call(
        body, grid=(m // tm, nk, nj),
        in_specs=[pl.BlockSpec((tm, tk), lambda i, k, j: (i, k), pipeline_mode=pl.Buffered(1)),
                  pl.BlockSpec((None,) * len(lead) + (tk, tn), lambda i, k, j: lead + (k, j)),
                  pl.BlockSpec((tm, tn), lambda i, k, j: (i, col(j, k))),
                  pl.BlockSpec((g, tn), lambda i, k, j: (rb(i), goff + col(j, k)))],
        out_specs=pl.BlockSpec((tm, tn), lambda i, k, j: (i, col(j, k))),
        out_shape=jax.ShapeDtypeStruct((m, d), F32),
        scratch_shapes=[pltpu.VMEM((nj, tm, tn), F32)], name=name,
        compiler_params=_cp("parallel", "arbitrary", "arbitrary"))(a, w, x, mod)


def _adanorm(cfg, x, norm_g3, gidx, mod, sub):
    m, d = x.shape
    tm = _pick_tm(cfg, 256)
    g = cfg.dbatch
    rb = _rowblk(cfg, tm)

    def body(x_ref, g_ref, sh_ref, sc_ref, o_ref):
        chunks = [slice(c * LANES, (c + 1) * LANES) for c in range(d // LANES)]
        acc = jnp.zeros((tm, LANES), F32)
        for sl in chunks:
            xc = x_ref[:, sl]
            acc = acc + xc * xc
        inv = lax.rsqrt(jnp.sum(acc, axis=-1, keepdims=True) * (1.0 / d) + EPS)
        for sl in chunks:
            y = x_ref[:, sl] * inv * g_ref[:, sl]
            y3 = y.reshape(tm // g, g, LANES) * (1.0 + sc_ref[:, sl])[None] + sh_ref[:, sl][None]
            o_ref[:, sl] = y3.reshape(tm, LANES).astype(BF16)

    return pl.pallas_call(
        body, grid=(m // tm,),
        in_specs=[pl.BlockSpec((tm, d), lambda i: (i, 0)),
                  pl.BlockSpec((None, 1, d), lambda i: (gidx, 0, 0)),
                  pl.BlockSpec((g, d), lambda i: (rb(i), 3 * sub)),
                  pl.BlockSpec((g, d), lambda i: (rb(i), 3 * sub + 1))],
        out_specs=pl.BlockSpec((tm, d), lambda i: (i, 0)),
        out_shape=jax.ShapeDtypeStruct((m, d), BF16),
        compiler_params=_cp("parallel"))(x, norm_g3, mod, mod)


def _layernorm_silu(y, ln_g, ln_b, lead):
    m, c = y.shape
    tm = _pick(m, 256, 8)

    def body(y_ref, g_ref, b_ref, o_ref):
        v = y_ref[...]
        vc = v - jnp.mean(v, axis=-1, keepdims=True)
        n = vc * lax.rsqrt(jnp.mean(vc * vc, axis=-1, keepdims=True) + EPS) * g_ref[...] + b_ref[...]
        o_ref[...] = _silu(n).astype(BF16)

    return pl.pallas_call(
        body, grid=(m // tm,),
        in_specs=[pl.BlockSpec((tm, c), lambda i: (i, 0)),
                  pl.BlockSpec((None, 1, c), lambda i: (lead, 0, 0)),
                  pl.BlockSpec((None, 1, c), lambda i: (lead, 0, 0))],
        out_specs=pl.BlockSpec((tm, c), lambda i: (i, 0)),
        out_shape=jax.ShapeDtypeStruct((m, c), BF16),
        compiler_params=_cp("parallel"))(y, ln_g, ln_b)


def _dwconv(x, init, w, w_lead, *, n_ch, row_off_blocks, batch, rows, tstride, tt, tc, post, out_dtype):
    taps = w.shape[-2]
    halo = (taps - 1) * tstride
    hp = init.shape[1]
    nt = rows // tt
    ncb = n_ch // tc

    def body(x_ref, init_ref, w_ref, o_ref, ext_ref):
        t = pl.program_id(2)

        @pl.when(t == 0)
        def _():
            ext_ref[0:hp, :] = init_ref[...]

        @pl.when(t > 0)
        def _():
            ext_ref[0:hp, :] = ext_ref[tt:tt + hp, :]

        ext_ref[hp:hp + tt, :] = x_ref[...]
        wv = w_ref[...]
        acc = None
        if tstride == 1 and taps > 8:
            for r_ in range(8):
                n_m = (taps - 1 - r_) // 8 + 1
                o = hp - halo + r_
                win = ext_ref[o:o + tt + 8 * (n_m - 1), :]
                for m_ in range(n_m):
                    term = win[8 * m_:8 * m_ + tt, :] * wv[r_ + 8 * m_:r_ + 8 * m_ + 1, :]
                    acc = term if acc is None else acc + term
        else:
            for j in range(taps):
                o = hp - halo + j * tstride
                term = ext_ref[o:o + tt, :] * wv[j:j + 1, :]
                acc = term if acc is None else acc + term
        o_ref[...] = post(acc, pl.program_id(1)).astype(out_dtype)

    return pl.pallas_call(
        body, grid=(batch, ncb, nt),
        in_specs=[pl.BlockSpec((tt, tc), lambda b, c, t: (row_off_blocks + b * nt + t, c)),
                  pl.BlockSpec((None, hp, tc), lambda b, c, t: (b, 0, c)),
                  pl.BlockSpec((None,) * len(w_lead) + (taps, tc), lambda b, c, t: w_lead + (0, c))],
        out_specs=pl.BlockSpec((tt, tc), lambda b, c, t: (b * nt + t, c)),
        out_shape=jax.ShapeDtypeStruct((batch * rows, n_ch), out_dtype),
        scratch_shapes=[pltpu.VMEM((hp + tt, tc), F32)],
        compiler_params=_cp("parallel", "parallel", "arbitrary"))(x, init, w)


def _dwconv_both(cfg, x, w, w_lead, n_ch, init_prompt, init_sample, post, out_dtype, tt_p, tc):
    tt_p = _pick(cfg.seq, tt_p, 8)
    tc = _pick(n_ch, tc)
    ms = cfg.dseq * cfg.dbatch
    yp = _dwconv(x, init_prompt, w, w_lead, n_ch=n_ch, row_off_blocks=0, batch=cfg.batch, rows=cfg.seq,
                 tstride=1, tt=tt_p, tc=tc, post=post, out_dtype=out_dtype)
    assert cfg.m_prompt % ms == 0
    ys = _dwconv(x, init_sample, w, w_lead, n_ch=n_ch, row_off_blocks=cfg.m_prompt // ms, batch=1, rows=ms,
                 tstride=cfg.dbatch, tt=ms, tc=tc, post=post, out_dtype=out_dtype)
    return yp, ys


def _halo_inits(cfg, state, taps, n_ch):
    h = taps - 1
    hp_p = -(-h // 8) * 8
    init_p = jnp.zeros((cfg.batch, hp_p, n_ch), F32)
    init_s = jnp.swapaxes(state.astype(F32), 0, 1).reshape(1, h * cfg.dbatch, n_ch)
    return init_p, init_s


def _delta_rule(qkv, z, ba, s0, s0_lead, gparams, norm_g, *, heads, seg, n_states, steps, hpb, state_out,
                prev_states=None, name="delta_rule"):
    rows = qkv.shape[0]
    r = LANES
    nseg = r // seg
    assert n_states in (1, nseg) and (n_states == 1 or steps == 1) and (n_states == nseg or nseg == 1)
    assert heads % hpb == 0
    groups = rows // (r * steps)
    z_arr, z_off = z
    ns = n_states
    n_layers, layer = state_out
    aliased = prev_states is not None

    def body(*refs):
        q_ref, k_ref, v_ref, z_ref, ba_ref, s0_ref, gp_ref, ng_ref = refs[:8]
        o_ref, sf_ref, s_ref = refs[-3:]
        hblk = pl.program_id(1)
        ci = pl.program_id(2)

        @pl.when(ci == 0)
        def _():
            for hi in range(hpb):
                s_ref[hi * ns:(hi + 1) * ns] = s0_ref[:, hi].astype(F32)

        bav = ba_ref[...]
        gp = gp_ref[...]
        lane = lax.broadcasted_iota(jnp.int32, (r, LANES), 1)
        beta_all = jax.nn.sigmoid(bav)
        xs = bav + gp[1:2, :]
        softplus = jnp.maximum(xs, 0.0) + jnp.log1p(jnp.exp(-jnp.abs(xs)))
        g_all = -jnp.exp(gp[0:1, :]) * softplus
        row = lax.broadcasted_iota(jnp.int32, (r, r), 0)
        col = lax.broadcasted_iota(jnp.int32, (r, r), 1)
        same = (row // seg) == (col // seg)
        incl = jnp.logical_and(same, row >= col)
        strict = jnp.logical_and(same, row > col)
        last = jnp.logical_and(same, col % seg == seg - 1)
        eye = jnp.where(row == col, 1.0, 0.0)
        offs = []
        s = 1
        while s < seg:
            rb = row // s
            offs.append(jnp.logical_and(rb % 2 == 1, (col // s) == rb - 1))
            s *= 2
        rseg = lax.broadcasted_iota(jnp.int32, (r, LANES), 0) // seg
        ng = ng_ref[...]

        hs = range(hpb)
        ls = [slice(hi * LANES, (hi + 1) * LANES) for hi in hs]
        q = [q_ref[:, ls[hi]] for hi in hs]
        k = [k_ref[:, ls[hi]] for hi in hs]
        v = [v_ref[:, ls[hi]] for hi in hs]
        beta = [jnp.sum(jnp.where(lane == hblk * hpb + hi, beta_all, 0.0), axis=-1, keepdims=True) for hi in hs]
        g = [jnp.sum(jnp.where(lane == hblk * hpb + hi + heads, g_all, 0.0), axis=-1, keepdims=True) for hi in hs]
        gc = [jnp.sum(jnp.where(incl, jnp.broadcast_to(g[hi], (r, r)).T, 0.0), axis=-1, keepdims=True)
              for hi in hs]
        gi = [jnp.broadcast_to(gc[hi], (r, r)) for hi in hs]
        gj = [gi[hi].T for hi in hs]
        decay = [jnp.where(incl, jnp.exp(jnp.where(incl, gi[hi] - gj[hi], 0.0)), 0.0) for hi in hs]
        kb = [k[hi] * beta[hi] for hi in hs]
        kk = [_bdot_nt(kb[hi], k[hi]) for hi in hs]
        m_mat = [jnp.where(strict, kk[hi] * decay[hi], 0.0) for hi in hs]
        t_inv = [eye - jnp.where(offs[0], m_mat[hi], 0.0) for hi in hs] if offs else [eye for _ in hs]
        for off in offs[1:]:
            left = [_bdot(t_inv[hi], jnp.where(off, m_mat[hi], 0.0)) for hi in hs]
            corr = [_bdot(left[hi], t_inv[hi]) for hi in hs]
            t_inv = [t_inv[hi] - corr[hi] for hi in hs]
        u = [_bdot(t_inv[hi], v[hi] * beta[hi]) for hi in hs]
        w = [_bdot(t_inv[hi], kb[hi] * jnp.exp(gi[hi])) for hi in hs]
        attn = [_bdot_nt(q[hi], k[hi]) * decay[hi] for hi in hs]
        qg = [q[hi] * jnp.exp(gi[hi]) for hi in hs]
        gl = [jnp.sum(jnp.where(last, gj[hi], 0.0), axis=-1, keepdims=True) for hi in hs]
        kd_t = [(k[hi] * jnp.exp(gl[hi] - gc[hi])).T for hi in hs]
        egl = [jnp.exp(gl[hi]) for hi in hs]
        if ns == 1:
            st = [s_ref[hi] for hi in hs]
            ws = [_bdot(w[hi], st[hi]) for hi in hs]
            dlt = [u[hi] - ws[hi] for hi in hs]
            qs = [_bdot(qg[hi], st[hi]) for hi in hs]
            ad = [_bdot(attn[hi], dlt[hi]) for hi in hs]
            o = [qs[hi] + ad[hi] for hi in hs]
            upd = [_bdot(kd_t[hi], dlt[hi]) for hi in hs]
            for hi in hs:
                s_ref[hi] = st[hi] * egl[hi][r - 1:r, :] + upd[hi]
        else:
            segs = range(nseg)
            wq = [[_bdot(jnp.concatenate([w[hi][si * seg:(si + 1) * seg], qg[hi][si * seg:(si + 1) * seg]],
                                         axis=0), s_ref[hi * ns + si]) for si in segs] for hi in hs]
            dlt = [u[hi] - jnp.concatenate([x[:seg] for x in wq[hi]], axis=0) for hi in hs]
            ad = [_bdot(attn[hi], dlt[hi]) for hi in hs]
            o = [jnp.concatenate([x[seg:] for x in wq[hi]], axis=0) + ad[hi] for hi in hs]
            for hi in hs:
                for si in segs:
                    upd = _bdot(kd_t[hi], jnp.where(rseg == si, dlt[hi], 0.0))
                    s_ref[hi * ns + si] = s_ref[hi * ns + si] * egl[hi][si * seg:si * seg + 1, :] + upd
        for hi in hs:
            on = o[hi] * lax.rsqrt(jnp.mean(o[hi] * o[hi], axis=-1, keepdims=True) + EPS) * ng
            o_ref[:, ls[hi]] = (on * _silu(z_ref[:, ls[hi]])).astype(BF16)

        @pl.when(ci == steps - 1)
        def _():
            for hi in range(hpb):
                sf_ref[:, hi] = s_ref[hi * ns:(hi + 1) * ns]

    hh = heads // hpb
    wl = hpb * LANES
    zb = z_off // hpb
    assert z_off % hpb == 0
    in_specs = [pl.BlockSpec((r, wl), lambda b, h, ci: (b * steps + ci, h)),
                pl.BlockSpec((r, wl), lambda b, h, ci: (b * steps + ci, hh + h)),
                pl.BlockSpec((r, wl), lambda b, h, ci: (b * steps + ci, 2 * hh + h)),
                pl.BlockSpec((r, wl), lambda b, h, ci: (b * steps + ci, zb + h)),
                pl.BlockSpec((r, LANES), lambda b, h, ci: (b * steps + ci, 0)),
                pl.BlockSpec((None,) * len(s0_lead) + (ns, hpb, LANES, LANES),
                             lambda b, h, ci: s0_lead + (b, h, 0, 0)),
                pl.BlockSpec((8, LANES), lambda b, h, ci: (0, 0)),
                pl.BlockSpec((1, LANES), lambda b, h, ci: (0, 0))]
    args = [qkv, qkv, qkv, z_arr, ba, s0, gparams, norm_g]
    aliases = {}
    if aliased:
        in_specs.append(pl.BlockSpec(memory_space=pl.ANY))
        args.append(prev_states)
        aliases = {len(args) - 1: 1}
    return pl.pallas_call(
        body, grid=(groups, heads // hpb, steps), in_specs=in_specs,
        out_specs=[pl.BlockSpec((r, wl), lambda b, h, ci: (b * steps + ci, h)),
                   pl.BlockSpec((None, ns, hpb, LANES, LANES), lambda b, h, ci: (layer, b, h, 0, 0))],
        out_shape=[jax.ShapeDtypeStruct((rows, heads * LANES), BF16),
                   jax.ShapeDtypeStruct((n_layers, groups * ns, heads, LANES, LANES), F32)],
        scratch_shapes=[pltpu.VMEM((hpb * ns, LANES, LANES), F32)],
        input_output_aliases=aliases, name=name,
        compiler_params=_cp("parallel", "parallel", "arbitrary"))(*args)


def _rope_tables(cfg):
    half = cfg.rope // 2
    inv = ROPE_THETA ** (-jnp.arange(0, cfg.rope, 2, dtype=F32) / cfg.rope)
    pos = jnp.concatenate([jnp.tile(jnp.arange(cfg.seq), cfg.batch),
                           jnp.repeat(cfg.n_pages * cfg.page + jnp.arange(cfg.dseq), cfg.dbatch)])
    ang = pos.astype(F32)[:, None] * inv[None, :]
    cos, sin = jnp.cos(ang), jnp.sin(ang)
    zpad = jnp.zeros((cfg.m, LANES - cfg.rope), F32)
    zh = jnp.zeros((cfg.m, half), F32)
    cosf = jnp.concatenate([cos, cos, zpad], axis=1)
    sin_a = jnp.concatenate([-sin, zh, zpad], axis=1)
    sin_b = jnp.concatenate([zh, sin, zpad], axis=1)
    return cosf, sin_a, sin_b


def _rope128(x, cosf, sin_a, sin_b, half):
    return (x * cosf + pltpu.roll(x, LANES - half, 1) * sin_a + pltpu.roll(x, half, 1) * sin_b)


def _mla_cq(cfg, h, w_dq, qa_g3, i):
    m = h.shape[0]
    n = cfg.q_lora
    tm = _pick(m, 1024, 8)
    tk = _pick(cfg.d, 512)

    def ep(accs, e_refs, o_refs):
        y = accs[0]
        y = y * lax.rsqrt(jnp.mean(y * y, axis=-1, keepdims=True) + EPS) * e_refs[0][...]
        o_refs[0][...] = y.astype(BF16)

    return _mm(h, [(w_dq, (i,), 0)], grid_m=m // tm, n_blocks=1, tm=tm, tn=n, tk=tk,
               outs=[((m, n), BF16, (tm, n), lambda i_, j: (i_, 0))],
               extras=[(qa_g3, (None, 1, n), lambda i_, j: (i, 0, 0))], epilogue=ep)[0]


def _mla_q(cfg, cq, w_uq_p, gq, gk, tables):
    m = cq.shape[0]
    hw = 2 * LANES
    tm = _pick(m, 1024, 8)
    qk_dim = cfg.nope + cfg.rope
    half = cfg.rope // 2
    scale = qk_dim ** -0.5

    def ep(accs, e_refs, o_refs):
        cosf, sa, sb, gq_r, gk_r = [r[...] for r in e_refs]
        acc = accs[0]
        rot = _rope128(acc[:, LANES:], cosf, sa, sb, half)
        qf = jnp.concatenate([acc[:, :LANES], rot], axis=-1)
        ssq = jnp.sum(qf * qf, axis=-1, keepdims=True) * (1.0 / qk_dim)
        o_refs[0][...] = (qf * lax.rsqrt(ssq + EPS) * gq_r * (gk_r * scale)).astype(BF16)

    tab = [(t, (tm, LANES), lambda i, j: (i, 0)) for t in tables]
    gains = [(g, (1, hw), lambda i, j: (0, 0)) for g in (gq, gk)]
    return _mm(cq, [(w_uq_p, (), 0)], grid_m=m // tm, n_blocks=cfg.heads, tm=tm, tn=hw, tk=cfg.q_lora,
               outs=[((m, cfg.heads * hw), BF16, (tm, hw), lambda i, j: (i, j))],
               extras=tab + gains, epilogue=ep)[0]


def _mla_ckv(cfg, h, w_dkv_p, kva_g3, i, tables):
    m = h.shape[0]
    kv = cfg.kv_lora
    n = kv + LANES
    tm = _pick(m, 512, 8)
    tk = _pick(cfg.d, 512)
    half = cfg.rope // 2

    def ep(accs, e_refs, o_refs):
        cosf, sa, sb, g = [r[...] for r in e_refs]
        acc = accs[0]
        c = acc[:, :kv]
        o_refs[0][...] = c * lax.rsqrt(jnp.mean(c * c, axis=-1, keepdims=True) + EPS) * g
        o_refs[1][...] = _rope128(acc[:, kv:], cosf, sa, sb, half)

    tab = [(t, (tm, LANES), lambda i_, j: (i_, 0)) for t in tables]
    return _mm(h, [(w_dkv_p, (), 0)], grid_m=m // tm, n_blocks=1, tm=tm, tn=n, tk=tk,
               outs=[((m, kv), F32, (tm, kv), lambda i_, j: (i_, 0)),
                     ((m, LANES), F32, (tm, LANES), lambda i_, j: (i_, 0))],
               extras=tab + [(kva_g3, (None, 1, kv), lambda i_, j: (i, 0, 0))], epilogue=ep)


def _mla_prompt_kv(cfg, ckv, krope_p, w_ukv, i):
    hw = 2 * LANES
    mp = cfg.m_prompt
    tm = _pick(mp, 1024, 8)
    qk_dim = cfg.nope + cfg.rope

    def ep(accs, e_refs, o_refs):
        acc = accs[0]
        kr = e_refs[0][...]
        kn = acc[:, :LANES]
        ssq = (jnp.sum(kn * kn, axis=-1, keepdims=True) + jnp.sum(kr * kr, axis=-1, keepdims=True))
        s = lax.rsqrt(ssq * (1.0 / qk_dim) + EPS)
        o_refs[0][...] = jnp.concatenate([kn * s, kr * s], axis=-1).astype(BF16)
        o_refs[1][...] = acc[:, LANES:].astype(BF16)

    return _mm(ckv, [(w_ukv, (i,), 0)], grid_m=mp // tm, n_blocks=cfg.heads, tm=tm, tn=hw, tk=cfg.kv_lora,
               outs=[((mp, cfg.heads * hw), BF16, (tm, hw), lambda i_, j: (i_, j)),
                     ((mp, cfg.heads * LANES), BF16, (tm, LANES), lambda i_, j: (i_, j))],
               extras=[(krope_p, (tm, LANES), lambda i_, j: (i_, 0))], epilogue=ep)


def _flash_prompt(cfg, q, k, v):
    tq = _pick(cfg.seq, 512, 8)
    nq = cfg.seq // tq
    hw = 2 * LANES
    hpb = _pick(cfg.heads, 4, 1)
    hs = range(hpb)

    def body(q_ref, k_ref, v_ref, o_ref, m_ref, l_ref, acc_ref):
        qi = pl.program_id(2)
        kj = pl.program_id(3)

        @pl.when(kj == 0)
        def _():
            m_ref[...] = jnp.full(m_ref.shape, NEG, F32)
            l_ref[...] = jnp.zeros(l_ref.shape, F32)
            acc_ref[...] = jnp.zeros(acc_ref.shape, F32)

        @pl.when(kj <= qi)
        def _():
            qpos = qi * tq + lax.broadcasted_iota(jnp.int32, (tq, tq), 0)
            kpos = kj * tq + lax.broadcasted_iota(jnp.int32, (tq, tq), 1)
            visible = kpos <= qpos
            s = [lax.dot_general(q_ref[:, h * hw:(h + 1) * hw], k_ref[:, h * hw:(h + 1) * hw],
                                 (((1,), (1,)), ((), ())), preferred_element_type=F32) for h in hs]
            s = [jnp.where(visible, s[h], NEG) for h in hs]
            m_prev = [m_ref[h] for h in hs]
            m_new = [jnp.maximum(m_prev[h], jnp.max(s[h], axis=-1, keepdims=True)) for h in hs]
            alpha = [jnp.exp(m_prev[h] - m_new[h]) for h in hs]
            p = [jnp.exp(s[h] - m_new[h]) for h in hs]
            pv = [jnp.dot(p[h].astype(BF16), v_ref[:, h * LANES:(h + 1) * LANES], preferred_element_type=F32)
                  for h in hs]
            for h in hs:
                l_ref[h] = alpha[h] * l_ref[h] + jnp.sum(p[h], axis=-1, keepdims=True)
                acc_ref[h] = alpha[h] * acc_ref[h] + pv[h]
                m_ref[h] = m_new[h]

        @pl.when(kj == nq - 1)
        def _():
            for h in hs:
                o_ref[:, h * LANES:(h + 1) * LANES] = (acc_ref[h] / l_ref[h]).astype(BF16)

    return pl.pallas_call(
        body, grid=(cfg.batch, cfg.heads // hpb, nq, nq),
        in_specs=[pl.BlockSpec((tq, hpb * hw), lambda b, h, qi, kj: (b * nq + qi, h)),
                  pl.BlockSpec((tq, hpb * hw), lambda b, h, qi, kj: (b * nq + jnp.minimum(kj, qi), h)),
                  pl.BlockSpec((tq, hpb * LANES), lambda b, h, qi, kj: (b * nq + jnp.minimum(kj, qi), h))],
        out_specs=pl.BlockSpec((tq, hpb * LANES), lambda b, h, qi, kj: (b * nq + qi, h)),
        out_shape=jax.ShapeDtypeStruct((cfg.m_prompt, cfg.heads * LANES), BF16),
        scratch_shapes=[pltpu.VMEM((hpb, tq, 1), F32), pltpu.VMEM((hpb, tq, 1), F32),
                        pltpu.VMEM((hpb, tq, LANES), F32)],
        name="flash_prompt", compiler_params=_cp("parallel", "parallel", "parallel", "arbitrary"))(q, k, v)


def _head_mm(a, w, *, a_block_of_head, kdim, n, out_dtype):
    rows = a.shape[0]
    heads = w.shape[0]

    def body(a_ref, w_ref, o_ref):
        o_ref[...] = jnp.dot(a_ref[...].astype(BF16), w_ref[...], preferred_element_type=F32).astype(out_dtype)

    return pl.pallas_call(
        body, grid=(heads,),
        in_specs=[pl.BlockSpec((rows, kdim), lambda h: (0, a_block_of_head(h))),
                  pl.BlockSpec((None, kdim, n), lambda h: (h, 0, 0))],
        out_specs=pl.BlockSpec((rows, n), lambda h: (0, h)),
        out_shape=jax.ShapeDtypeStruct((rows, heads * n), out_dtype),
        compiler_params=_cp("parallel"))(a, w)


def _decode_attention(cfg, page_table, cache_ckv, cache_krope_t, layer, qabs, qr, c_new, kr_new_t, w_uk_t):
    db, hq, kv = qabs.shape
    tq = cfg.dseq
    heads = cfg.heads
    npg = cfg.n_pages
    ppk = _pick(npg, 8, 2)
    assert npg % ppk == 0 and ppk % 2 == 0 and hq == heads * tq and tq == 8
    nsteps = npg // ppk
    page = cfg.page
    sub = 2 * page
    qk_dim = cfg.nope + cfg.rope
    hb_rows = min(heads, 8) * cfg.nope

    def body(pt_ref, *refs):
        c_refs = refs[:ppk]
        r_refs = refs[ppk:2 * ppk]
        qa_ref, qr_ref, cn_ref, rn_ref, wt_ref, o_ref, m_ref, l_ref, acc_ref, s_ref = refs[2 * ppk:]
        p = pl.program_id(1)

        @pl.when(p == 0)
        def _():
            m_ref[...] = jnp.full(m_ref.shape, NEG, F32)
            l_ref[...] = jnp.zeros(l_ref.shape, F32)
            acc_ref[...] = jnp.zeros(acc_ref.shape, F32)

        def process(cb, kr_t, new_rows):
            nkeys = cb.shape[0]
            ssq_r = jnp.sum(kr_t * kr_t, axis=0, keepdims=True)
            s_all = _bdot_nt(qa_ref[...], cb) + _bdot(qr_ref[...], kr_t)
            for lo in range(0, nkeys, sub):
                hi_ = min(lo + sub, nkeys)
                for hb in range(heads * cfg.nope // hb_rows):
                    kt = _bdot_nt(wt_ref[hb * hb_rows:(hb + 1) * hb_rows, :], cb[lo:hi_])
                    for hh in range(hb_rows // cfg.nope):
                        hd = hb * (hb_rows // cfg.nope) + hh
                        sl = kt[hh * cfg.nope:(hh + 1) * cfg.nope, :]
                        ssq = jnp.sum(sl * sl, axis=0, keepdims=True) + ssq_r[:, lo:hi_]
                        ksc = lax.rsqrt(ssq * (1.0 / qk_dim) + EPS)
                        s_ref[hd * tq:(hd + 1) * tq, lo:hi_] = s_all[hd * tq:(hd + 1) * tq, lo:hi_] * ksc
            s = s_ref[:, 0:nkeys]
            if new_rows:
                qrow = lax.broadcasted_iota(jnp.int32, (hq, nkeys), 0) % tq
                kcol = lax.broadcasted_iota(jnp.int32, (hq, nkeys), 1)
                s = jnp.where(kcol <= qrow, s, NEG)
            m_prev = m_ref[...]
            m_new = jnp.maximum(m_prev, jnp.max(s, axis=-1, keepdims=True))
            alpha = jnp.exp(m_prev - m_new)
            pr = jnp.exp(s - m_new)
            l_ref[...] = alpha * l_ref[...] + jnp.sum(pr, axis=-1, keepdims=True)
            acc_ref[...] = alpha * acc_ref[...] + jnp.dot(pr.astype(BF16), cb, preferred_element_type=F32)
            m_ref[...] = m_new

        @pl.when(p < nsteps)
        def _():
            process(jnp.concatenate([c[...].astype(BF16) for c in c_refs], axis=0),
                    jnp.concatenate([r_[...] for r_ in r_refs], axis=1), False)

        @pl.when(p == nsteps)
        def _():
            process(cn_ref[...].astype(BF16), rn_ref[...], True)
            o_ref[...] = acc_ref[...] / l_ref[...]

    def pg(which):
        def imap(b, p, pt):
            pp = jnp.minimum(p, nsteps - 1)
            return (layer, pt[b * npg + ppk * pp + which], 0, 0)
        return imap

    grid_spec = pltpu.PrefetchScalarGridSpec(
        num_scalar_prefetch=1, grid=(db, nsteps + 1),
        in_specs=[pl.BlockSpec((None, None, page, kv), pg(w_)) for w_ in range(ppk)]
        + [pl.BlockSpec((None, None, cfg.rope, page), pg(w_)) for w_ in range(ppk)]
        + [pl.BlockSpec((None, hq, kv), lambda b, p, pt: (b, 0, 0)),
                  pl.BlockSpec((None, hq, cfg.rope), lambda b, p, pt: (b, 0, 0)),
                  pl.BlockSpec((None, LANES, kv), lambda b, p, pt: (b, 0, 0)),
                  pl.BlockSpec((None, cfg.rope, LANES), lambda b, p, pt: (b, 0, 0)),
                  pl.BlockSpec((heads * cfg.nope, kv), lambda b, p, pt: (0, 0))],
        out_specs=pl.BlockSpec((None, hq, kv), lambda b, p, pt: (b, 0, 0)),
        scratch_shapes=[pltpu.VMEM((hq, 1), F32), pltpu.VMEM((hq, 1), F32), pltpu.VMEM((hq, kv), F32),
                        pltpu.VMEM((hq, ppk * page), F32)])
    return pl.pallas_call(
        body, grid_spec=grid_spec, out_shape=jax.ShapeDtypeStruct((db, hq, kv), F32),
        name="decode_attention", compiler_params=_cp("parallel", "arbitrary"))(
            page_table.reshape(-1), *([cache_ckv] * ppk), *([cache_krope_t] * ppk),
            qabs, qr, c_new, kr_new_t, w_uk_t)


def _to_batch_major(cfg, rows):
    return jnp.swapaxes(rows.reshape(cfg.dseq, cfg.dbatch, rows.shape[-1]), 0, 1)


def _to_time_major(cfg, arr):
    return jnp.swapaxes(arr, 0, 1).reshape(cfg.dseq * cfg.dbatch, arr.shape[-1])


def _last_rows_prompt(cfg, rows, n_last, n_ch):
    return jnp.stack([lax.slice(rows, ((b + 1) * cfg.seq - n_last, 0), ((b + 1) * cfg.seq, n_ch))
                      for b in range(cfg.batch)])


def _last_rows_sample(cfg, rows, state, n_last, n_ch):
    mp, g, t = cfg.m_prompt, cfg.dbatch, cfg.dseq
    keep = max(n_last - t, 0)
    new = lax.slice(rows, (mp + max(t - n_last, 0) * g, 0), (mp + t * g, n_ch))
    new = jnp.swapaxes(new.reshape(-1, g, n_ch), 0, 1)
    if keep == 0:
        return new
    return jnp.concatenate([state[:, n_last - keep:].astype(F32), new], axis=1)


def _ffn(cfg, x, mod, norm_g3, l, sub, which, wg, wu, wd):
    h = _adanorm(cfg, x, norm_g3, 3 * l + sub, mod, sub)
    m, d = x.shape
    dff = wg.shape[-1]
    tm = _pick(m, 1024, 8)
    tn = _pick(dff, 512)
    tk = _pick(d, 4096)
    act = _mm(h, [(wg, (l, which), 0), (wu, (l, which), 0)], grid_m=m // tm, n_blocks=dff // tn,
              tm=tm, tn=tn, tk=tk, outs=[((m, dff), BF16, (tm, tn), lambda i, j: (i, j))],
              epilogue=_ep_swiglu, name="ffn_up", a_single=True)[0]
    return _resid_mm(cfg, act, wd, (l, which), x, mod, 3 * sub + 2, 0.5, name="ffn_down")


def _deltanet_layer(cfg, h, i, state_delta, state_delta_conv, dn_w_in, dn_conv_w, dn_a_log, dn_dt_bias,
                    dn_norm_g, prompt_chunk, prev_p, prev_s):
    heads, dk = cfg.heads, cfg.dk
    d_a = heads * dk
    n_qkvz = 4 * d_a
    mp = cfg.m_prompt
    w_in_t = jnp.swapaxes(dn_w_in, 1, 2)
    proj = _plain_mm(cfg, h, w_in_t, (i,), n=n_qkvz, w_nt=True, name="dn_in_proj")
    w_ba_t = jnp.pad(w_in_t[i, n_qkvz:, :], ((0, LANES - 2 * heads), (0, 0)))
    ba = _plain_mm(cfg, h, w_ba_t, (), n=LANES, tn=LANES, w_nt=True, name="dn_ba_proj")

    nq = d_a
    tc = _pick(3 * d_a, 512)

    def post(y, cb):
        y = _silu(y)
        is_qk = cb < (2 * nq) // tc
        is_q = cb < nq // tc
        outs = []
        for s in range(tc // dk):
            ys = y[:, s * dk:(s + 1) * dk]
            ssq = jnp.sum(ys * ys, axis=-1, keepdims=True)
            sc = jnp.where(is_qk, lax.rsqrt(ssq + EPS), 1.0) * jnp.where(is_q, dk ** -0.5, 1.0)
            outs.append(ys * sc)
        return jnp.concatenate(outs, axis=-1)

    taps = dn_conv_w.shape[1]
    init_p, init_s = _halo_inits(cfg, state_delta_conv[i], taps, 3 * d_a)
    qkv_p, qkv_s = _dwconv_both(cfg, proj, dn_conv_w, (i,), 3 * d_a, init_p, init_s, post, F32, 256, tc)

    gparams = jnp.zeros((8, LANES), F32)
    gparams = gparams.at[0, heads:2 * heads].set(dn_a_log[i].astype(F32))
    gparams = gparams.at[1, heads:2 * heads].set(dn_dt_bias[i].astype(F32))
    ng = dn_norm_g[i].astype(F32).reshape(1, dk)

    s0_p = jnp.zeros((cfg.batch, heads, dk, dk), F32)
    n_layers = state_delta.shape[0]
    o_p, s_p = _delta_rule(qkv_p, (proj, 3 * heads), ba, s0_p, (), gparams, ng,
                           heads=heads, seg=prompt_chunk, n_states=1, steps=cfg.seq // LANES,
                           hpb=_pick(heads, 8, 1), state_out=(n_layers, i), prev_states=prev_p,
                           name="delta_rule_prompt")
    ms = cfg.dseq * cfg.dbatch
    o_s, s_s = _delta_rule(_to_batch_major(cfg, qkv_s).reshape(ms, 3 * d_a),
                           (_to_batch_major(cfg, proj[mp:, 3 * d_a:]).reshape(ms, d_a), 0),
                           _to_batch_major(cfg, ba[mp:]).reshape(ms, LANES), state_delta, (i,), gparams, ng,
                           heads=heads, seg=cfg.dseq, n_states=LANES // cfg.dseq, steps=1,
                           hpb=_pick(heads, 2, 1), state_out=(n_layers, i), prev_states=prev_s,
                           name="delta_rule_sample")
    o_all = jnp.concatenate([o_p, _to_time_major(cfg, o_s.reshape(cfg.dbatch, cfg.dseq, d_a))], axis=0)
    new_conv_p = _last_rows_prompt(cfg, proj, taps - 1, 3 * d_a)
    new_conv_s = _last_rows_sample(cfg, proj, state_delta_conv[i], taps - 1, 3 * d_a)
    return o_all, s_p, s_s, new_conv_p, new_conv_s


def _conformer_layer(cfg, h, i, state_conv, cv_w_pw1, cv_b_pw1, cv_w_dw, cv_b_dw, cv_ln_g, cv_ln_b):
    m, d = h.shape
    dc = cv_w_dw.shape[-1]
    tm = _pick(m, 1024, 8)
    tn = _pick(dc, 256)
    tk = _pick(d, 4096)
    b3 = cv_b_pw1.reshape(cv_b_pw1.shape[0], 1, 2 * dc)
    u = _mm(h, [(cv_w_pw1, (i,), 0), (cv_w_pw1, (i,), dc // tn)], grid_m=m // tm, n_blocks=dc // tn,
            tm=tm, tn=tn, tk=tk, outs=[((m, dc), F32, (tm, tn), lambda i_, j: (i_, j))],
            extras=[(b3, (None, 1, tn), lambda i_, j: (i, 0, j)),
                    (b3, (None, 1, tn), lambda i_, j: (i, 0, dc // tn + j))],
            epilogue=_ep_glu, name="conformer_glu")[0]
    taps = cv_w_dw.shape[1]
    init_p, init_s = _halo_inits(cfg, state_conv[i], taps, dc)
    yp, ys = _dwconv_both(cfg, u, cv_w_dw, (i,), dc, init_p, init_s, lambda y, cb: y, F32, 128, 256)
    y = jnp.concatenate([yp, ys], axis=0) + cv_b_dw[i][None, :]
    hn = _layernorm_silu(y, cv_ln_g.reshape(-1, 1, dc), cv_ln_b.reshape(-1, 1, dc), i)
    new_p = _last_rows_prompt(cfg, u, taps - 1, dc)
    new_s = _last_rows_sample(cfg, u, state_conv[i], taps - 1, dc)
    return hn, new_p, new_s


def _mla_layer(cfg, h, i, cache_ckv, cache_krope, page_table, tables, m_w_dq, m_qa_g, m_w_uq, m_w_dkv,
               m_kva_g, m_w_ukv, m_q_norm_g, m_k_norm_g):
    heads, nope, rope, kv = cfg.heads, cfg.nope, cfg.rope, cfg.kv_lora
    qk_dim = nope + rope
    hw = 2 * LANES
    mp = cfg.m_prompt
    half = rope // 2

    def tied(g):
        g = g.astype(F32)
        return jnp.concatenate([g[:nope], g[nope:], g[nope:], jnp.zeros((hw - qk_dim,), F32)]).reshape(1, hw)

    w_uq_p = jnp.pad(m_w_uq[i].reshape(cfg.q_lora, heads, qk_dim),
                     ((0, 0), (0, 0), (0, hw - qk_dim))).reshape(cfg.q_lora, heads * hw)
    w_dkv_p = jnp.pad(m_w_dkv[i], ((0, 0), (0, LANES - rope)))
    cq = _mla_cq(cfg, h, m_w_dq, m_qa_g.reshape(-1, 1, cfg.q_lora), i)
    q = _mla_q(cfg, cq, w_uq_p, tied(m_q_norm_g[i]), tied(m_k_norm_g[i]), tables)
    ckv, krope_p = _mla_ckv(cfg, h, w_dkv_p, m_kva_g.reshape(-1, 1, kv), i, tables)

    k_p, v_p = _mla_prompt_kv(cfg, ckv, krope_p, m_w_ukv, i)
    att_p = _flash_prompt(cfg, q, k_p, v_p)

    w3 = m_w_ukv[i].reshape(kv, heads, nope + LANES)
    w_uk_h = jnp.transpose(w3[:, :, :nope], (1, 2, 0)).astype(BF16)
    w_uv_h = jnp.transpose(w3[:, :, nope:], (1, 0, 2)).astype(BF16)
    q_s = q[mp:]
    ms = q_s.shape[0]
    qabs = _head_mm(q_s, w_uk_h, a_block_of_head=lambda hd: 2 * hd, kdim=nope, n=kv, out_dtype=BF16)
    qabs = jnp.transpose(qabs.reshape(cfg.dseq, cfg.dbatch, heads, kv), (1, 2, 0, 3)).reshape(
        cfg.dbatch, heads * cfg.dseq, kv)
    qr = q_s.reshape(cfg.dseq, cfg.dbatch, heads, hw)[..., nope:qk_dim]
    qr = jnp.transpose(qr, (1, 2, 0, 3)).reshape(cfg.dbatch, heads * cfg.dseq, rope)
    ckv_s = _to_batch_major(cfg, ckv[mp:])
    kr_s = _to_batch_major(cfg, krope_p[mp:, :rope])
    c_new = jnp.pad(ckv_s, ((0, 0), (0, LANES - cfg.dseq), (0, 0)))
    kr_new_t = jnp.pad(jnp.swapaxes(kr_s, 1, 2), ((0, 0), (0, 0), (0, LANES - cfg.dseq)))
    o_lat = _decode_attention(cfg, page_table, cache_ckv, jnp.swapaxes(cache_krope, 2, 3), i, qabs, qr, c_new,
                              kr_new_t,
                              w_uk_h.reshape(heads * nope, kv))
    o_lat = jnp.transpose(o_lat.reshape(cfg.dbatch, heads, cfg.dseq, kv), (2, 0, 1, 3)).reshape(ms, heads * kv)
    att_s = _head_mm(o_lat, w_uv_h, a_block_of_head=lambda hd: hd, kdim=kv, n=LANES, out_dtype=BF16)
    att = jnp.concatenate([att_p, att_s], axis=0)
    ckv_p_out = ckv[:mp].reshape(cfg.batch, cfg.seq, kv)
    kr_p_out = krope_p[:mp, :rope].reshape(cfg.batch, cfg.seq, rope)
    return att, ckv_p_out, kr_p_out, ckv_s, kr_s


def kernel(x_prompt, x_sample, cache_ckv, cache_krope, state_delta, state_delta_conv, state_conv, page_table,
           c_prompt, c_sample, ada_w, ada_b, norm_g, ffn_wg, ffn_wu, ffn_wd, dn_w_in, dn_conv_w, dn_a_log,
           dn_dt_bias, dn_norm_g, dn_w_out, cv_w_pw1, cv_b_pw1, cv_w_dw, cv_b_dw, cv_ln_g, cv_ln_b, cv_w_pw2,
           cv_b_pw2, m_w_dq, m_qa_g, m_w_uq, m_w_dkv, m_kva_g, m_w_ukv, m_q_norm_g, m_k_norm_g, m_w_o):
    batch, seq, d = x_prompt.shape
    dbatch, dseq, _ = x_sample.shape
    depth = ada_w.shape[0]
    rope = cache_krope.shape[-1]
    kv_lora = cache_ckv.shape[-1]
    heads = dn_a_log.shape[-1]
    nope = m_q_norm_g.shape[-1] - rope // 2
    cfg = Cfg(batch=batch, seq=seq, dbatch=dbatch, dseq=dseq, d=d, m_prompt=batch * seq,
              m=batch * seq + dbatch * dseq, n_pages=page_table.shape[1], page=cache_ckv.shape[2],
              heads=heads, dk=state_delta.shape[-2], nope=nope, rope=rope, kv_lora=kv_lora,
              q_lora=m_w_dq.shape[-1])
    assert dbatch % 8 == 0 and seq % dbatch == 0 and cfg.dk == LANES and nope == LANES

    x = jnp.concatenate([x_prompt.reshape(batch * seq, d).astype(F32),
                         jnp.swapaxes(x_sample.astype(F32), 0, 1).reshape(dseq * dbatch, d)], axis=0)
    c_all = jnp.concatenate([jnp.repeat(c_prompt, dbatch, axis=0), c_sample], axis=0).astype(F32)
    norm_g3 = norm_g.reshape(depth * 3, 1, d).astype(F32)
    ada_b3 = ada_b.reshape(depth, 1, N_ADA * d)
    tables = _rope_tables(cfg)
    mrows = c_all.shape[0]
    tn_ada = _pick(N_ADA * d, 512)
    tk_ada = _pick(d, 4096)

    def silu_body(c_ref, o_ref):
        o_ref[...] = _silu(c_ref[...]).astype(BF16)

    c_act = pl.pallas_call(silu_body, out_shape=jax.ShapeDtypeStruct(c_all.shape, BF16), name="ada_silu")(c_all)

    def ep_bias(accs, e_refs, o_refs):
        o_refs[0][...] = accs[0] + e_refs[0][...]

    outs = {k: [] for k in ("ckv_p", "kr_p", "ckv_s", "kr_s", "dnc_p", "dnc_s", "cv_p", "cv_s")}
    dn_p = dn_s = None
    for l in range(depth):
        mod = _mm(c_act, [(ada_w, (l,), 0)], grid_m=1, n_blocks=N_ADA * d // tn_ada, tm=mrows, tn=tn_ada,
                  tk=tk_ada, outs=[((mrows, N_ADA * d), F32, (mrows, tn_ada), lambda i, j: (0, j))],
                  extras=[(ada_b3, (None, 1, tn_ada), lambda i, j, l=l: (l, 0, j))], epilogue=ep_bias,
                  name="ada")[0]
        x = _ffn(cfg, x, mod, norm_g3, l, 0, 0, ffn_wg, ffn_wu, ffn_wd)
        h = _adanorm(cfg, x, norm_g3, 3 * l + 1, mod, 1)
        kind, i = l % 3, l // 3
        if kind == 0:
            o, dn_p, dn_s, nc_p, nc_s = _deltanet_layer(cfg, h, i, state_delta, state_delta_conv, dn_w_in,
                                                        dn_conv_w, dn_a_log, dn_dt_bias, dn_norm_g,
                                                        min(LANES, seq), dn_p, dn_s)
            x = _resid_mm(cfg, o, dn_w_out, (i,), x, mod, 5, 1.0)
            outs["dnc_p"].append(nc_p)
            outs["dnc_s"].append(nc_s)
        elif kind == 1:
            hn, nb_p, nb_s = _conformer_layer(cfg, h, i, state_conv, cv_w_pw1, cv_b_pw1, cv_w_dw, cv_b_dw,
                                              cv_ln_g, cv_ln_b)
            x = _resid_mm(cfg, hn, cv_w_pw2, (i,), x, mod, 5, 1.0,
                          bias=cv_b_pw2.reshape(-1, 1, d), bias_lead=(i,))
            outs["cv_p"].append(nb_p)
            outs["cv_s"].append(nb_s)
        else:
            att, ckv_p, kr_p, ckv_s, kr_s = _mla_layer(cfg, h, i, cache_ckv, cache_krope, page_table, tables,
                                                       m_w_dq, m_qa_g, m_w_uq, m_w_dkv, m_kva_g, m_w_ukv,
                                                       m_q_norm_g, m_k_norm_g)
            x = _resid_mm(cfg, att, m_w_o, (i,), x, mod, 5, 1.0)
            outs["ckv_p"].append(ckv_p)
            outs["kr_p"].append(kr_p)
            outs["ckv_s"].append(ckv_s)
            outs["kr_s"].append(kr_s)
        x = _ffn(cfg, x, mod, norm_g3, l, 2, 1, ffn_wg, ffn_wu, ffn_wd)

    mp = cfg.m_prompt
    y_prompt = x[:mp].reshape(batch, seq, d)
    y_sample = jnp.swapaxes(x[mp:].reshape(dseq, dbatch, d), 0, 1)
    st = jnp.stack
    return (y_prompt, y_sample, st(outs["ckv_p"]), st(outs["kr_p"]), st(outs["ckv_s"]), st(outs["kr_s"]),
            dn_p, dn_s, st(outs["dnc_p"]), st(outs["dnc_s"]),
            st(outs["cv_p"]), st(outs["cv_s"]))
```
